```python
import math
import jax, jax.numpy as jnp
from jax import lax
import numpy as np

D_MODEL = 1024
BATCH = 16
SEQ = 4096
DEPTH = 1
DEC_BATCH = 128
DEC_SEQ = 8
PAST_LEN = 8192
PAGE_SIZE = 128

N_META = 16
RW_HEAD_DIM = 64
RW_WIDTH = D_MODEL // 2
RW_HEADS = RW_WIDTH // RW_HEAD_DIM
RW_DECAY_RANK = 64
RW_ICLR_RANK = 64
RW_GATE_RANK = 128
RW_SHIFT_WIDTH = 3 * RW_WIDTH + RW_DECAY_RANK + RW_ICLR_RANK + RW_GATE_RANK
RW_GN_EPS = RW_HEAD_DIM * 1e-5
DA_HEAD_DIM = 64
DA_HEADS = (D_MODEL // 2) // (2 * DA_HEAD_DIM)
DA_QK_WIDTH = DA_HEADS * 2 * DA_HEAD_DIM
DA_V_DIM = 2 * DA_HEAD_DIM
DA_V_WIDTH = DA_HEADS * DA_V_DIM
Q_BLOCK = 128
IN_WIDTH = RW_SHIFT_WIDTH + 2 * DA_QK_WIDTH + DA_V_WIDTH + 2 * D_MODEL
N_EXPERTS = 32
TOP_K = 4
D_FF_EXPERT = D_MODEL
SWIGLU_LIMIT = 7.0
SWIGLU_ALPHA = 1.702
MOE_BLOCK = 128
NORM_EPS = 1e-5
NEG_INF = -1e30
POOL_NUM = 5
POOL_DEN = 4

kernel_name = 'hybrid_rwkv7_diffattn_moe_step'


def rmsnorm(x, w):
    xf = x.astype(jnp.float32)
    y = xf * lax.rsqrt(jnp.mean(xf * xf, axis=-1, keepdims=True) + NORM_EPS)
    return (y * w.astype(jnp.float32)).astype(x.dtype)


def layer_params(l, **arrays):
    return {name: arr[l] for name, arr in arrays.items()}


def wkv_scan(r, w, k, v, kk, a, s0):
    def step(s, inp):
        r_t, w_t, k_t, v_t, kk_t, a_t = inp
        sa = jnp.einsum('bhij,bhj->bhi', s, -kk_t)
        s = s * w_t[:, :, None, :] + sa[..., None] * (kk_t * a_t)[:, :, None, :] + v_t[..., None] * k_t[:, :, None, :]
        return s, jnp.einsum('bhij,bhj->bhi', s, r_t)
    xs = tuple(t.swapaxes(0, 1) for t in (r, w, k, v, kk, a))
    s, ys = lax.scan(step, s0, xs)
    return ys.swapaxes(0, 1), s


def rwkv7_branch(pr, shift_prev, wkv0, p):
    B, T, _ = pr.shape
    prev = jnp.concatenate([shift_prev[:, None].astype(pr.dtype), pr[:, :-1]], axis=1)
    xs = pr + p['rw_mu'] * (prev - pr)
    cuts = [RW_WIDTH, 2 * RW_WIDTH, 3 * RW_WIDTH, 3 * RW_WIDTH + RW_DECAY_RANK,
            3 * RW_WIDTH + RW_DECAY_RANK + RW_ICLR_RANK]
    r, k, v, xw, xa, xg = jnp.split(xs, cuts, axis=-1)
    z = (p['rw_w0'] + jnp.tanh(xw) @ p['rw_w2']).astype(jnp.float32)
    decay = jnp.exp(-jnp.exp(-jax.nn.softplus(-z) - 0.5))
    a = jax.nn.sigmoid((p['rw_a0'] + xa @ p['rw_a2']).astype(jnp.float32))
    g = jax.nn.sigmoid(xg) @ p['rw_g2']
    heads = lambda t: t.astype(jnp.float32).reshape(B, T, RW_HEADS, RW_HEAD_DIM)
    chan = lambda t: t.astype(jnp.float32).reshape(RW_HEADS, RW_HEAD_DIM)
    r_h, k_h, v_h, w_h, a_h, g_h = heads(r), heads(k), heads(v), heads(decay), heads(a), heads(g)
    kk = k_h * chan(p['rw_k_k'])
    kk = kk / jnp.maximum(jnp.sqrt(jnp.sum(kk * kk, axis=-1, keepdims=True)), 1e-12)
    k_h = k_h * (1.0 + (a_h - 1.0) * chan(p['rw_k_a']))
    y, s = wkv_scan(r_h, w_h, k_h, v_h, kk, a_h, wkv0.astype(jnp.float32))
    mu = jnp.mean(y, axis=-1, keepdims=True)
    var = jnp.mean(jnp.square(y - mu), axis=-1, keepdims=True)
    yn = (y - mu) * lax.rsqrt(var + RW_GN_EPS) * chan(p['rw_ln_w']) + chan(p['rw_ln_b'])
    bonus = jnp.sum(r_h * k_h * chan(p['rw_r_k']), axis=-1, keepdims=True) * v_h
    out = ((yn + bonus) * g_h).reshape(B, T, RW_WIDTH).astype(pr.dtype)
    return out, s.astype(pr.dtype), pr[:, -1]


def diff_attend_block(q, q_pos, k, v, k_pos, lam):
    scale = DA_HEAD_DIM ** -0.5
    mask = k_pos[None, :] <= q_pos[:, None]
    def attn_map(qh, kh):
        s = jnp.einsum('bqhd,bkhd->bhqk', qh, kh).astype(jnp.float32) * scale
        return jax.nn.softmax(jnp.where(mask, s, NEG_INF), axis=-1)
    amap = attn_map(q[..., :DA_HEAD_DIM], k[..., :DA_HEAD_DIM]) - lam * attn_map(q[..., DA_HEAD_DIM:], k[..., DA_HEAD_DIM:])
    return jnp.einsum('bhqk,bkhe->bqhe', amap.astype(v.dtype), v)


def prompt_attend(q, k, v, lam):
    B, L = q.shape[:2]
    pos = jnp.arange(L)
    o_meta = diff_attend_block(q[:, :N_META], pos[:N_META], k, v, pos, lam)
    n_blk = (L - N_META) // Q_BLOCK
    qb = q[:, N_META:].reshape(B, n_blk, Q_BLOCK, DA_HEADS, 2 * DA_HEAD_DIM).swapaxes(0, 1)
    pb = pos[N_META:].reshape(n_blk, Q_BLOCK)
    o = lax.map(lambda a: diff_attend_block(a[0], a[1], k, v, pos, lam), (qb, pb))
    o = o.swapaxes(0, 1).reshape(B, L - N_META, DA_HEADS, DA_V_DIM)
    return jnp.concatenate([o_meta, o], axis=1)


def make_sample_attend(cache_k, cache_v, page_table):
    def sample_attend(q, k, v, lam):
        T = q.shape[1]
        past = page_table.shape[1] * cache_k.shape[1]
        k_pos = jnp.arange(past + T)
        q_pos = past + jnp.arange(T)
        def per_seq(args):
            qs, ks, vs, pt = args
            k_all = jnp.concatenate([cache_k[pt].reshape(past, DA_HEADS, 2 * DA_HEAD_DIM).astype(ks.dtype), ks], axis=0)
            v_all = jnp.concatenate([cache_v[pt].reshape(past, DA_HEADS, DA_V_DIM).astype(vs.dtype), vs], axis=0)
            return diff_attend_block(qs[None], q_pos, k_all[None], v_all[None], k_pos, lam)[0]
        return lax.map(per_seq, (q, k, v, page_table))
    return sample_attend


def moe(h, p):
    T, D = h.shape
    logits = (h @ p['w_router'] + p['b_router']).astype(jnp.float32)
    top_v, top_i = lax.top_k(logits, TOP_K)
    gates = jax.nn.softmax(top_v, axis=-1).astype(h.dtype)
    A = T * TOP_K
    flat_e = top_i.reshape(-1)
    flat_tok = jnp.repeat(jnp.arange(T), TOP_K)
    flat_g = gates.reshape(-1)
    order = jnp.argsort(flat_e)
    se, stok, sg = flat_e[order], flat_tok[order], flat_g[order]
    counts = jnp.bincount(flat_e, length=N_EXPERTS)
    starts = jnp.cumsum(counts) - counts
    padded = (counts + MOE_BLOCK - 1) // MOE_BLOCK * MOE_BLOCK
    pad_end = jnp.cumsum(padded)
    pad_start = pad_end - padded
    dest = pad_start[se] + jnp.arange(A) - starts[se]
    n_blk = -(-A // MOE_BLOCK) + N_EXPERTS
    rows = jnp.zeros((n_blk * MOE_BLOCK, D), h.dtype).at[dest].set(h[stok])
    blk_e = jnp.minimum(jnp.searchsorted(pad_end, jnp.arange(n_blk) * MOE_BLOCK, side='right'), N_EXPERTS - 1)
    def expert_block(args):
        xb, e = args
        gu = xb @ p['w_gu'][e] + p['b_gu'][e]
        gate, up = gu[:, :D_FF_EXPERT], gu[:, D_FF_EXPERT:]
        gate = jnp.minimum(gate, SWIGLU_LIMIT)
        up = jnp.clip(up, -SWIGLU_LIMIT, SWIGLU_LIMIT)
        glu = gate * jax.nn.sigmoid(gate * SWIGLU_ALPHA)
        return ((up + 1.0) * glu) @ p['w_down'][e] + p['b_down'][e]
    out = lax.map(expert_block, (rows.reshape(n_blk, MOE_BLOCK, D), blk_e)).reshape(n_blk * MOE_BLOCK, D)
    return jax.ops.segment_sum(out[dest] * sg[:, None], stok, num_segments=T)


def trunk_layer(x, shift_prev, wkv0, attend, p, lam_init):
    B, T, _ = x.shape
    h = rmsnorm(x, p['norm1_w'])
    proj = h @ p['w_in']
    c1 = RW_SHIFT_WIDTH
    c2 = c1 + DA_QK_WIDTH
    c3 = c2 + DA_QK_WIDTH
    c4 = c3 + DA_V_WIDTH
    pr, pq, pk, pv, pg = jnp.split(proj, [c1, c2, c3, c4], axis=-1)
    ya, wkv_new, shift_new = rwkv7_branch(pr, shift_prev, wkv0, p)
    q = pq.reshape(B, T, DA_HEADS, 2 * DA_HEAD_DIM)
    k = pk.reshape(B, T, DA_HEADS, 2 * DA_HEAD_DIM)
    v = pv.reshape(B, T, DA_HEADS, DA_V_DIM)
    f = lambda t: t.astype(jnp.float32)
    lam = jnp.exp(jnp.sum(f(p['da_lq1']) * f(p['da_lk1']))) - jnp.exp(jnp.sum(f(p['da_lq2']) * f(p['da_lk2']))) + lam_init
    o = attend(q, k, v, lam)
    o = (rmsnorm(o, p['da_subln_w']) * (1.0 - lam_init)).reshape(B, T, DA_V_WIDTH)
    gate_a, gate_b = jnp.split(jax.nn.sigmoid(pg), 2, axis=-1)
    merged = gate_a * (ya @ p['w_out_rwkv']) + gate_b * (o @ p['w_out_diff'])
    x = x + merged @ p['w_out']
    x = x + moe(rmsnorm(x, p['norm2_w']).reshape(B * T, D_MODEL), p).reshape(B, T, D_MODEL)
    return x, k, v, wkv_new, shift_new


def setup_inputs(seed: int = 0) -> dict:
    key = jax.random.key(seed)
    ks = iter(jax.random.split(key, 48))
    nrm = lambda shape, scale: jax.random.normal(next(ks), shape, jnp.float32) * scale
    n_pages = PAST_LEN // PAGE_SIZE
    n_pool = DEC_BATCH * n_pages * POOL_NUM // POOL_DEN
    perm = jax.random.permutation(next(ks), n_pool)
    page_table = perm[:DEC_BATCH * n_pages].reshape(DEC_BATCH, n_pages).astype(jnp.int32)
    L = DEPTH
    return {
        'x_prompt': nrm((BATCH, SEQ, D_MODEL), 1.0),
        'x_sample': nrm((DEC_BATCH, DEC_SEQ, D_MODEL), 1.0),
        'cache_k': nrm((L, n_pool, PAGE_SIZE, DA_HEADS, 2 * DA_HEAD_DIM), 1.0),
        'cache_v': nrm((L, n_pool, PAGE_SIZE, DA_HEADS, DA_V_DIM), 1.0),
        'state_wkv': nrm((L, DEC_BATCH, RW_HEADS, RW_HEAD_DIM, RW_HEAD_DIM), 0.3),
        'state_shift': nrm((L, DEC_BATCH, RW_SHIFT_WIDTH), 1.0),
        'page_table': page_table,
        'meta_tokens': nrm((N_META, D_MODEL), 1.0),
        'norm1_w': 1.0 + nrm((L, D_MODEL), 0.02),
        'w_in': nrm((L, D_MODEL, IN_WIDTH), D_MODEL ** -0.5),
        'rw_mu': jax.random.uniform(next(ks), (L, RW_SHIFT_WIDTH), jnp.float32),
        'rw_w0': nrm((L, RW_WIDTH), 0.5),
        'rw_w2': nrm((L, RW_DECAY_RANK, RW_WIDTH), 0.1),
        'rw_a0': nrm((L, RW_WIDTH), 0.5),
        'rw_a2': nrm((L, RW_ICLR_RANK, RW_WIDTH), 0.5 * RW_ICLR_RANK ** -0.5),
        'rw_g2': nrm((L, RW_GATE_RANK, RW_WIDTH), RW_GATE_RANK ** -0.5),
        'rw_k_k': 0.85 + nrm((L, RW_WIDTH), 0.05),
        'rw_k_a': 1.0 + nrm((L, RW_WIDTH), 0.05),
        'rw_r_k': nrm((L, RW_WIDTH), 0.1),
        'rw_ln_w': 1.0 + nrm((L, RW_WIDTH), 0.02),
        'rw_ln_b': nrm((L, RW_WIDTH), 0.02),
        'w_out_rwkv': nrm((L, RW_WIDTH, D_MODEL), RW_WIDTH ** -0.5),
        'da_lq1': nrm((L, DA_HEAD_DIM), 0.1),
        'da_lk1': nrm((L, DA_HEAD_DIM), 0.1),
        'da_lq2': nrm((L, DA_HEAD_DIM), 0.1),
        'da_lk2': nrm((L, DA_HEAD_DIM), 0.1),
        'da_subln_w': 1.0 + nrm((L, DA_V_DIM), 0.02),
        'w_out_diff': nrm((L, DA_V_WIDTH, D_MODEL), DA_V_WIDTH ** -0.5),
        'w_out': nrm((L, D_MODEL, D_MODEL), D_MODEL ** -0.5),
        'norm2_w': 1.0 + nrm((L, D_MODEL), 0.02),
        'w_router': nrm((L, D_MODEL, N_EXPERTS), D_MODEL ** -0.5),
        'b_router': nrm((L, N_EXPERTS), 0.01),
        'w_gu': nrm((L, N_EXPERTS, D_MODEL, 2 * D_FF_EXPERT), D_MODEL ** -0.5),
        'b_gu': nrm((L, N_EXPERTS, 2 * D_FF_EXPERT), 0.01),
        'w_down': nrm((L, N_EXPERTS, D_FF_EXPERT, D_MODEL), D_FF_EXPERT ** -0.5),
        'b_down': nrm((L, N_EXPERTS, D_MODEL), 0.01),
        'norm_f_w': 1.0 + nrm((D_MODEL,), 0.02),
    }


def reference(x_prompt, x_sample, cache_k, cache_v, state_wkv, state_shift, page_table, meta_tokens,
              norm1_w, w_in, rw_mu, rw_w0, rw_w2, rw_a0, rw_a2, rw_g2, rw_k_k, rw_k_a, rw_r_k,
              rw_ln_w, rw_ln_b, w_out_rwkv, da_lq1, da_lk1, da_lq2, da_lk2, da_subln_w, w_out_diff,
              w_out, norm2_w, w_router, b_router, w_gu, b_gu, w_down, b_down, norm_f_w):
    B = x_prompt.shape[0]
    meta = jnp.broadcast_to(meta_tokens.astype(x_prompt.dtype)[None], (B, N_META, D_MODEL))
    xp = jnp.concatenate([meta, x_prompt], axis=1)
    xs = x_sample
    kp_l, vp_l, wp_l, sp_l, ks_l, vs_l, ws_l, ss_l = [], [], [], [], [], [], [], []
    for l in range(DEPTH):
        p = layer_params(l, norm1_w=norm1_w, w_in=w_in, rw_mu=rw_mu, rw_w0=rw_w0, rw_w2=rw_w2,
                         rw_a0=rw_a0, rw_a2=rw_a2, rw_g2=rw_g2, rw_k_k=rw_k_k, rw_k_a=rw_k_a,
                         rw_r_k=rw_r_k, rw_ln_w=rw_ln_w, rw_ln_b=rw_ln_b, w_out_rwkv=w_out_rwkv,
                         da_lq1=da_lq1, da_lk1=da_lk1, da_lq2=da_lq2, da_lk2=da_lk2,
                         da_subln_w=da_subln_w, w_out_diff=w_out_diff, w_out=w_out, norm2_w=norm2_w,
                         w_router=w_router, b_router=b_router, w_gu=w_gu, b_gu=b_gu,
                         w_down=w_down, b_down=b_down)
        lam_init = 0.8 - 0.6 * math.exp(-0.3 * l)
        shift0 = jnp.zeros((B, RW_SHIFT_WIDTH), xp.dtype)
        wkv0 = jnp.zeros((B, RW_HEADS, RW_HEAD_DIM, RW_HEAD_DIM), xp.dtype)
        xp, kp, vp, wp, sp = trunk_layer(xp, shift0, wkv0, prompt_attend, p, lam_init)
        attend_s = make_sample_attend(cache_k[l], cache_v[l], page_table)
        xs, ksm, vsm, wsm, ssm = trunk_layer(xs, state_shift[l], state_wkv[l], attend_s, p, lam_init)
        kp_l.append(kp); vp_l.append(vp); wp_l.append(wp); sp_l.append(sp)
        ks_l.append(ksm); vs_l.append(vsm); ws_l.append(wsm); ss_l.append(ssm)
    y_prompt = rmsnorm(xp[:, N_META:], norm_f_w)
    y_sample = rmsnorm(xs, norm_f_w)
    return (y_prompt, y_sample, jnp.stack(kp_l), jnp.stack(vp_l), jnp.stack(wp_l), jnp.stack(sp_l),
            jnp.stack(ks_l), jnp.stack(vs_l), jnp.stack(ws_l), jnp.stack(ss_l))
```

```python
import functools
import math

import jax
import jax.numpy as jnp
from jax import lax
from jax.experimental import pallas as pl
from jax.experimental.pallas import tpu as pltpu

F32 = jnp.float32
BF16 = jnp.bfloat16

N_META = 16
RW_HEAD_DIM = 64
DA_HEAD_DIM = 64
TOP_K = 4
SWIGLU_LIMIT = 7.0
SWIGLU_ALPHA = 1.702
NORM_EPS = 1e-5
RW_GN_EPS = RW_HEAD_DIM * 1e-5
NEG_INF = -1e30
KK_EPS = 1e-12

LANES = 128
TOKEN_TILE = 256
ATTN_TILE = 256
SCAN_CHUNK = 16
PAGES_PER_STEP = 8
MOE_ROWS = 512
VMEM_LIMIT = 56 * 1024 * 1024


def _params(sem):
    return pltpu.CompilerParams(dimension_semantics=sem, vmem_limit_bytes=VMEM_LIMIT)


def _rms(x, w):
    return x * lax.rsqrt(jnp.mean(x * x, axis=-1, keepdims=True) + NORM_EPS) * w


def _full(shape):
    n = len(shape)
    return pl.BlockSpec(shape, lambda *_: (0,) * n)


def _inproj_kernel(seq_len, has_start, rw, rd, ri, rg, qk, vw, *refs):
    if has_start:
        (x_ref, n1_ref, w_ref, mu_ref, w0_ref, w2_ref, a0_ref, a2_ref, g2_ref, sm_ref, sv_ref,
         pr_ref, r_ref, dec_ref, kx_ref, vx_ref, a_ref, g_ref, k_ref, v_ref, qb_ref, kb_ref, vb_ref,
         last_scr) = refs
    else:
        (x_ref, n1_ref, w_ref, mu_ref, w0_ref, w2_ref, a0_ref, a2_ref, g2_ref,
         pr_ref, r_ref, dec_ref, kx_ref, vx_ref, a_ref, g_ref, k_ref, v_ref, qb_ref, kb_ref, vb_ref,
         last_scr) = refs
    i = pl.program_id(0)
    tm = x_ref.shape[0]
    c1 = 3 * rw + rd + ri + rg

    @pl.when(i == 0)
    def _():
        last_scr[...] = jnp.zeros_like(last_scr)

    h = _rms(x_ref[...], n1_ref[...])
    proj = jnp.dot(h.astype(BF16), w_ref[...], preferred_element_type=F32)
    pr = proj[:, :c1]
    pr_ref[...] = pr
    q = proj[:, c1:c1 + qk]
    k = proj[:, c1 + qk:c1 + 2 * qk]
    v = proj[:, c1 + 2 * qk:c1 + 2 * qk + vw]
    k_ref[...] = k
    v_ref[...] = v
    qb_ref[...] = (q * (DA_HEAD_DIM ** -0.5)).astype(BF16)
    kb_ref[...] = k.astype(BF16)
    vb_ref[...] = v.astype(BF16)

    row = lax.broadcasted_iota(jnp.int32, (tm, 1), 0)
    prev = jnp.where(row == 0, last_scr[...], pltpu.roll(pr, 1, 0))
    if has_start:
        prev = jnp.where(sm_ref[...] > 0.0, sv_ref[...], prev)
    else:
        first = lax.rem(seq_len - lax.rem(i * tm, seq_len), seq_len)
        prev = jnp.where(row == first, 0.0, prev)
    last_scr[...] = pr[tm - 1:tm, :]
    xs = pr + mu_ref[...] * (prev - pr)

    r_ref[...] = xs[:, :rw]
    kx_ref[...] = xs[:, rw:2 * rw]
    vx_ref[...] = xs[:, 2 * rw:3 * rw]
    xw = xs[:, 3 * rw:3 * rw + rd]
    xa = xs[:, 3 * rw + rd:3 * rw + rd + ri]
    xg = xs[:, 3 * rw + rd + ri:c1]
    z = w0_ref[...] + jnp.dot(jnp.tanh(xw).astype(BF16), w2_ref[...], preferred_element_type=F32)
    dec_ref[...] = jnp.exp(-math.exp(-0.5) * jax.nn.sigmoid(z))
    a_ref[...] = jax.nn.sigmoid(
        a0_ref[...] + jnp.dot(xa.astype(BF16), a2_ref[...], preferred_element_type=F32))
    g_ref[...] = jnp.dot(jax.nn.sigmoid(xg).astype(BF16), g2_ref[...], preferred_element_type=F32)


def _in_projection(x, seq_len, start_mask, start_vals, n1, w_main, mu, w0, w2, a0, a2, g2, dims):
    rw, rd, ri, rg, qk, vw = dims
    n, d = x.shape
    tm = min(TOKEN_TILE, n)
    assert n % tm == 0
    c1 = 3 * rw + rd + ri + rg
    has_start = start_mask is not None
    tok = lambda w: pl.BlockSpec((tm, w), lambda i: (i, 0))
    in_specs = [tok(d), _full(n1.shape), _full(w_main.shape), _full(mu.shape), _full(w0.shape),
                _full(w2.shape), _full(a0.shape), _full(a2.shape), _full(g2.shape)]
    args = [x, n1, w_main, mu, w0, w2, a0, a2, g2]
    if has_start:
        in_specs += [tok(1), tok(c1)]
        args += [start_mask, start_vals]
    out_shape = ([jax.ShapeDtypeStruct((n, c1), F32)]
                 + [jax.ShapeDtypeStruct((n, rw), F32)] * 6
                 + [jax.ShapeDtypeStruct((n, qk), F32), jax.ShapeDtypeStruct((n, vw), F32),
                    jax.ShapeDtypeStruct((n, qk), BF16), jax.ShapeDtypeStruct((n, qk), BF16),
                    jax.ShapeDtypeStruct((n, vw), BF16)])
    out_specs = [tok(c1)] + [tok(rw)] * 6 + [tok(qk), tok(vw), tok(qk), tok(qk), tok(vw)]
    return pl.pallas_call(
        functools.partial(_inproj_kernel, seq_len, has_start, rw, rd, ri, rg, qk, vw),
        grid=(n // tm,), in_specs=in_specs, out_specs=out_specs, out_shape=out_shape,
        scratch_shapes=[pltpu.VMEM((1, c1), F32)],
        compiler_params=_params(("arbitrary",)), name="in_projection")(*args)


def _scan_kernel(r_ref, w_ref, kx_ref, v_ref, a_ref, g_ref, s0_ref,
                 kkp_ref, kap_ref, rkp_ref, lnw_ref, lnb_ref,
                 out_ref, s_ref, kk_scr, kka_scr, kp_scr, y_scr):
    t = pl.program_id(1)
    tc, n, _ = r_ref.shape

    @pl.when(t == 0)
    def _():
        s_ref[...] = s0_ref[...]

    kx = kx_ref[...]
    a = a_ref[...]
    kk = kx * kkp_ref[...]
    kk = kk / jnp.maximum(jnp.sqrt(jnp.sum(kk * kk, axis=1, keepdims=True)), KK_EPS)
    kk_scr[...] = kk
    kka_scr[...] = kk * a
    kp_scr[...] = kx * (1.0 + (a - 1.0) * kap_ref[...])

    def step(tt, carry):
        sa = jnp.zeros((n, LANES), F32)
        for j in range(n):
            sa = sa + s_ref[j] * kk_scr[tt, pl.ds(j, 1), :]
        sa = -sa
        vt = v_ref[tt]
        y = jnp.zeros((n, LANES), F32)
        for j in range(n):
            row = pl.ds(j, 1)
            sj = (s_ref[j] * w_ref[tt, row, :] + sa * kka_scr[tt, row, :]
                  + vt * kp_scr[tt, row, :])
            s_ref[j] = sj
            y = y + sj * r_ref[tt, row, :]
        y_scr[tt] = y
        return carry

    lax.fori_loop(0, tc, step, 0)

    y = y_scr[...]
    mu = jnp.mean(y, axis=1, keepdims=True)
    var = jnp.mean(jnp.square(y - mu), axis=1, keepdims=True)
    yn = (y - mu) * lax.rsqrt(var + RW_GN_EPS) * lnw_ref[...] + lnb_ref[...]
    bonus = jnp.sum(r_ref[...] * kp_scr[...] * rkp_ref[...], axis=1, keepdims=True) * v_ref[...]
    out_ref[...] = (yn + bonus) * g_ref[...]


def _wkv_scan(seqs, s0, lane_params):
    t_len, n, bh = seqs[0].shape
    tc = SCAN_CHUNK if t_len % SCAN_CHUNK == 0 else t_len
    assert t_len % tc == 0 and bh % LANES == 0
    seq_spec = pl.BlockSpec((tc, n, LANES), lambda l, t: (t, 0, l))
    st_spec = pl.BlockSpec((n, n, LANES), lambda l, t: (0, 0, l))
    par_spec = pl.BlockSpec((n, LANES), lambda l, t: (0, 0))
    return pl.pallas_call(
        _scan_kernel, grid=(bh // LANES, t_len // tc),
        in_specs=[seq_spec] * 6 + [st_spec] + [par_spec] * 5,
        out_specs=[seq_spec, st_spec],
        out_shape=[jax.ShapeDtypeStruct((t_len, n, bh), F32), jax.ShapeDtypeStruct((n, n, bh), F32)],
        scratch_shapes=[pltpu.VMEM((tc, n, LANES), F32)] * 4,
        compiler_params=_params(("arbitrary", "arbitrary")), name="wkv_scan")(*seqs, s0, *lane_params)


def _lambda(lq1_ref, lk1_ref, lq2_ref, lk2_ref, lam_init):
    s1 = jnp.sum(lq1_ref[...] * lk1_ref[...], axis=-1, keepdims=True)
    s2 = jnp.sum(lq2_ref[...] * lk2_ref[...], axis=-1, keepdims=True)
    return jnp.exp(s1) - jnp.exp(s2) + lam_init


def _online_update(s, v_blk, m_scr, l_scr, acc_scr):
    m_prev = m_scr[...]
    m_new = jnp.maximum(m_prev, jnp.max(s, axis=-1, keepdims=True))
    alpha = jnp.exp(m_prev - m_new)
    p = jnp.exp(s - m_new)
    l_scr[...] = alpha * l_scr[...] + jnp.sum(p, axis=-1, keepdims=True)
    acc_scr[...] = alpha * acc_scr[...] + jnp.dot(p.astype(BF16), v_blk, preferred_element_type=F32)
    m_scr[...] = m_new


def _prompt_attn_kernel(seq_len, lam_init, q_ref, k_ref, v_ref, lq1_ref, lk1_ref, lq2_ref, lk2_ref,
                        subw_ref, o_ref, qbd_scr, m_scr, l_scr, acc_scr):
    qi = pl.program_id(2)
    tq = q_ref.shape[1]
    tk = tq
    n_full = seq_len // tk
    tail = seq_len - n_full * tk

    q = q_ref[0]
    lane = lax.broadcasted_iota(jnp.int32, q.shape, 1)
    qbd_scr[0:tq, :] = jnp.where(lane < DA_HEAD_DIM, q, jnp.zeros_like(q))
    qbd_scr[tq:2 * tq, :] = jnp.where(lane >= DA_HEAD_DIM, q, jnp.zeros_like(q))
    m_scr[...] = jnp.full_like(m_scr, NEG_INF)
    l_scr[...] = jnp.zeros_like(l_scr)
    acc_scr[...] = jnp.zeros_like(acc_scr)

    def scores(k_blk):
        return lax.dot_general(qbd_scr[...], k_blk, (((1,), (1,)), ((), ())),
                               preferred_element_type=F32)

    def body(kb, carry):
        off = pl.multiple_of(kb * tk, tk)
        _online_update(scores(k_ref[0, pl.ds(off, tk), :]), v_ref[0, pl.ds(off, tk), :],
                       m_scr, l_scr, acc_scr)
        return carry

    lax.fori_loop(0, jnp.minimum(qi, n_full), body, 0)

    def causal(width):
        rowi = lax.broadcasted_iota(jnp.int32, (2 * tq, width), 0)
        coli = lax.broadcasted_iota(jnp.int32, (2 * tq, width), 1)
        return coli <= jnp.where(rowi >= tq, rowi - tq, rowi)

    @pl.when(qi < n_full)
    def _():
        off = pl.multiple_of(qi * tk, tk)
        s = jnp.where(causal(tk), scores(k_ref[0, pl.ds(off, tk), :]), NEG_INF)
        _online_update(s, v_ref[0, pl.ds(off, tk), :], m_scr, l_scr, acc_scr)

    if tail:
        @pl.when(qi == n_full)
        def _():
            s = jnp.where(causal(tail), scores(k_ref[0, n_full * tk:seq_len, :]), NEG_INF)
            _online_update(s, v_ref[0, n_full * tk:seq_len, :], m_scr, l_scr, acc_scr)

    lam = _lambda(lq1_ref, lk1_ref, lq2_ref, lk2_ref, lam_init)
    acc = acc_scr[...]
    l = l_scr[...]
    o = acc[:tq] / l[:tq] - lam * (acc[tq:] / l[tq:])
    o_ref[0] = (_rms(o, subw_ref[...]) * (1.0 - lam_init)).astype(o_ref.dtype)


def _prompt_attention(qb, kb, vb, lam_params, subw, lam_init, n_heads):
    b, seq_len, _ = qb.shape
    tq = ATTN_TILE
    dv = vb.shape[-1] // n_heads
    nq = pl.cdiv(seq_len, tq)
    small = [_full(p.shape) for p in lam_params] + [_full(subw.shape)]
    return pl.pallas_call(
        functools.partial(_prompt_attn_kernel, seq_len, lam_init),
        grid=(b, n_heads, nq),
        in_specs=[pl.BlockSpec((1, tq, 2 * DA_HEAD_DIM), lambda bi, h, qi: (bi, qi, h)),
                  pl.BlockSpec((1, seq_len, 2 * DA_HEAD_DIM), lambda bi, h, qi: (bi, 0, h)),
                  pl.BlockSpec((1, seq_len, dv), lambda bi, h, qi: (bi, 0, h))] + small,
        out_specs=pl.BlockSpec((1, tq, dv), lambda bi, h, qi: (bi, qi, h)),
        out_shape=jax.ShapeDtypeStruct((b, seq_len, n_heads * dv), BF16),
        scratch_shapes=[pltpu.VMEM((2 * tq, 2 * DA_HEAD_DIM), BF16), pltpu.VMEM((2 * tq, 1), F32),
                        pltpu.VMEM((2 * tq, 1), F32), pltpu.VMEM((2 * tq, dv), F32)],
        compiler_params=_params(("arbitrary",) * 3), name="prompt_attention")(
            qb, kb, vb, *lam_params, subw)


def _sample_attn_kernel(n_heads, lam_init, pt_ref, q_ref, kn_ref, vn_ref, lq1_ref, lk1_ref, lq2_ref,
                        lk2_ref, subw_ref, *refs):
    del pt_ref
    pp = PAGES_PER_STEP
    k_pages, v_pages = refs[:pp], refs[pp:2 * pp]
    o_ref, qbd_scr, m_scr, l_scr, acc_scr = refs[2 * pp:]
    j = pl.program_id(1)
    t_new, width = q_ref.shape[1], q_ref.shape[2]
    rows = 2 * n_heads * t_new
    dv = vn_ref.shape[2] // n_heads

    @pl.when(j == 0)
    def _():
        q = q_ref[0]
        qt = jnp.concatenate([q] * (2 * n_heads), axis=0)
        rowi = lax.broadcasted_iota(jnp.int32, (rows, width), 0)
        coli = lax.broadcasted_iota(jnp.int32, (rows, width), 1)
        head = lax.rem(rowi // t_new, n_heads)
        mp = rowi // (n_heads * t_new)
        keep = (coli // DA_HEAD_DIM) == (2 * head + mp)
        qbd_scr[...] = jnp.where(keep, qt, 0.0).astype(BF16)
        m_scr[...] = jnp.full_like(m_scr, NEG_INF)
        l_scr[...] = jnp.zeros_like(l_scr)
        acc_scr[...] = jnp.zeros_like(acc_scr)

    def scores(k_blk):
        return lax.dot_general(qbd_scr[...], k_blk, (((1,), (1,)), ((), ())),
                               preferred_element_type=F32)

    for p in range(pp):
        _online_update(scores(k_pages[p][0].astype(BF16)), v_pages[p][0].astype(BF16),
                       m_scr, l_scr, acc_scr)

    @pl.when(j == pl.num_programs(1) - 1)
    def _():
        nk = kn_ref.shape[1]
        rowi = lax.broadcasted_iota(jnp.int32, (rows, nk), 0)
        coli = lax.broadcasted_iota(jnp.int32, (rows, nk), 1)
        s = jnp.where(coli <= lax.rem(rowi, t_new), scores(kn_ref[0].astype(BF16)), NEG_INF)
        _online_update(s, vn_ref[0].astype(BF16), m_scr, l_scr, acc_scr)
        lam = _lambda(lq1_ref, lk1_ref, lq2_ref, lk2_ref, lam_init)
        on = acc_scr[...] / l_scr[...]
        half = n_heads * t_new
        for h in range(n_heads):
            cols = slice(h * dv, (h + 1) * dv)
            o = (on[h * t_new:(h + 1) * t_new, cols]
                 - lam * on[half + h * t_new:half + (h + 1) * t_new, cols])
            o_ref[0, :, cols] = _rms(o, subw_ref[...]) * (1.0 - lam_init)


def _sample_attention(q, k_new, v_new, cache_k, cache_v, page_table, lam_params, subw, lam_init,
                      n_heads):
    db, t_new, width = q.shape
    n_pages = page_table.shape[1]
    page = cache_k.shape[1]
    pp = PAGES_PER_STEP
    assert n_pages % pp == 0
    vw = cache_v.shape[2]
    rows = 2 * n_heads * t_new
    seq = lambda w, t: pl.BlockSpec((1, t, w), lambda b, j, pt: (b, 0, 0))
    small = [pl.BlockSpec(p.shape, lambda b, j, pt: (0, 0)) for p in (*lam_params, subw)]

    def page_spec(p, w):
        return pl.BlockSpec((1, page, w), lambda b, j, pt: (pt[b, j * pp + p], 0, 0))

    grid_spec = pltpu.PrefetchScalarGridSpec(
        num_scalar_prefetch=1, grid=(db, n_pages // pp),
        in_specs=[seq(width, t_new), seq(width, k_new.shape[1]), seq(vw, v_new.shape[1])] + small
        + [page_spec(p, width) for p in range(pp)] + [page_spec(p, vw) for p in range(pp)],
        out_specs=pl.BlockSpec((1, t_new, vw), lambda b, j, pt: (b, 0, 0)),
        scratch_shapes=[pltpu.VMEM((rows, width), BF16), pltpu.VMEM((rows, 1), F32),
                        pltpu.VMEM((rows, 1), F32), pltpu.VMEM((rows, vw), F32)])
    return pl.pallas_call(
        functools.partial(_sample_attn_kernel, n_heads, lam_init), grid_spec=grid_spec,
        out_shape=jax.ShapeDtypeStruct((db, t_new, vw), F32),
        compiler_params=_params(("arbitrary", "arbitrary")), name="sample_attention")(
            page_table, q, k_new, v_new, *lam_params, subw, *([cache_k] * pp), *([cache_v] * pp))


def _merge_kernel(n_experts, x_ref, ya_ref, o_ref, n1_ref, wg_ref, wr_ref, wd_ref, wo_ref, n2_ref,
                  rhi_ref, rlo_ref, rb_ref, x1_ref, h2_ref, route_ref):
    x = x_ref[...]
    d = x.shape[1]
    h = _rms(x, n1_ref[...])
    gates = jax.nn.sigmoid(jnp.dot(h.astype(BF16), wg_ref[...], preferred_element_type=F32))
    ya = jnp.dot(ya_ref[...].astype(BF16), wr_ref[...], preferred_element_type=F32)
    od = jnp.dot(o_ref[...].astype(BF16), wd_ref[...], preferred_element_type=F32)
    merged = gates[:, :d] * ya + gates[:, d:] * od
    x1 = x + jnp.dot(merged.astype(BF16), wo_ref[...], preferred_element_type=F32)
    x1_ref[...] = x1
    h2 = _rms(x1, n2_ref[...])
    h2_hi = h2.astype(BF16)
    h2_ref[...] = h2_hi
    h2_lo = (h2 - h2_hi.astype(F32)).astype(BF16)
    logits = (jnp.dot(h2_hi, rhi_ref[...], preferred_element_type=F32)
              + jnp.dot(h2_lo, rhi_ref[...], preferred_element_type=F32)
              + jnp.dot(h2_hi, rlo_ref[...], preferred_element_type=F32) + rb_ref[...])
    lane = lax.broadcasted_iota(jnp.int32, logits.shape, 1)
    work = jnp.where(lane < n_experts, logits, -jnp.inf)
    vals, idxs = [], []
    for _ in range(TOP_K):
        m = jnp.max(work, axis=-1, keepdims=True)
        idx = jnp.min(jnp.where(work == m, lane, LANES), axis=-1, keepdims=True)
        vals.append(m)
        idxs.append(idx)
        work = jnp.where(lane == idx, -jnp.inf, work)
    es = [jnp.exp(v - vals[0]) for v in vals]
    denom = es[0] + es[1] + es[2] + es[3]
    route = jnp.zeros(logits.shape, F32)
    for kk in range(TOP_K):
        route = jnp.where(lane == kk, es[kk] / denom, route)
        route = jnp.where(lane == TOP_K + kk, idxs[kk].astype(F32), route)
    route_ref[...] = route


def _merge_and_route(x, ya, o, n1, wg, wr, wd, wo, n2, rhi, rlo, rb, n_experts):
    n, d = x.shape
    tm = min(TOKEN_TILE, n)
    assert n % tm == 0
    tok = lambda w: pl.BlockSpec((tm, w), lambda i: (i, 0))
    weights = (n1, wg, wr, wd, wo, n2, rhi, rlo, rb)
    return pl.pallas_call(
        functools.partial(_merge_kernel, n_experts), grid=(n // tm,),
        in_specs=[tok(d), tok(ya.shape[1]), tok(o.shape[1])] + [_full(w.shape) for w in weights],
        out_specs=[tok(d), tok(d), tok(LANES)],
        out_shape=[jax.ShapeDtypeStruct((n, d), F32), jax.ShapeDtypeStruct((n, d), BF16),
                   jax.ShapeDtypeStruct((n, LANES), F32)],
        compiler_params=_params(("arbitrary",)), name="merge_and_route")(x, ya, o, *weights)


def _expert_kernel(blk_e_ref, n_used_ref, rows_ref, wgu_ref, bgu_ref, wdn_ref, bdn_ref, out_ref):
    del blk_e_ref
    i = pl.program_id(0)
    f = wdn_ref.shape[1]

    @pl.when(i < n_used_ref[0])
    def _():
        gu = jnp.dot(rows_ref[...], wgu_ref[0], preferred_element_type=F32) + bgu_ref[0]
        gate = jnp.minimum(gu[:, :f], SWIGLU_LIMIT)
        up = jnp.clip(gu[:, f:], -SWIGLU_LIMIT, SWIGLU_LIMIT)
        glu = gate * jax.nn.sigmoid(gate * SWIGLU_ALPHA)
        mid = ((up + 1.0) * glu).astype(BF16)
        out_ref[...] = jnp.dot(mid, wdn_ref[0], preferred_element_type=F32) + bdn_ref[0]

    @pl.when(i >= n_used_ref[0])
    def _():
        out_ref[...] = jnp.zeros_like(out_ref)


def _expert_blocks(rows, blk_e, n_used, wgu, bgu, wdn, bdn, bm):
    n_rows, d = rows.shape
    n_blk = n_rows // bm
    f2, f = wgu.shape[2], wdn.shape[1]
    grid_spec = pltpu.PrefetchScalarGridSpec(
        num_scalar_prefetch=2, grid=(n_blk,),
        in_specs=[pl.BlockSpec((bm, d), lambda i, be, nu: (i, 0)),
                  pl.BlockSpec((1, d, f2), lambda i, be, nu: (be[i], 0, 0)),
                  pl.BlockSpec((1, 1, f2), lambda i, be, nu: (be[i], 0, 0)),
                  pl.BlockSpec((1, f, d), lambda i, be, nu: (be[i], 0, 0)),
                  pl.BlockSpec((1, 1, d), lambda i, be, nu: (be[i], 0, 0))],
        out_specs=pl.BlockSpec((bm, d), lambda i, be, nu: (i, 0)))
    return pl.pallas_call(
        _expert_kernel, grid_spec=grid_spec, out_shape=jax.ShapeDtypeStruct((n_rows, d), F32),
        compiler_params=_params(("arbitrary",)), name="expert_blocks")(
            blk_e, n_used, rows, wgu, bgu, wdn, bdn)


def _moe(h2, route, wgu, bgu, wdn, bdn, n_experts):
    n, d = h2.shape
    a = n * TOP_K
    bm = min(MOE_ROWS, a // n_experts)
    gates = route[:, :TOP_K]
    flat_e = route[:, TOP_K:2 * TOP_K].astype(jnp.int32).reshape(-1)
    order = jnp.argsort(flat_e, stable=True)
    se = flat_e[order]
    counts = jnp.bincount(flat_e, length=n_experts)
    starts = jnp.cumsum(counts) - counts
    padded = (counts + bm - 1) // bm * bm
    pad_end = jnp.cumsum(padded)
    pad_start = pad_end - padded
    n_blk = -(-a // bm) + n_experts
    dest_sorted = pad_start[se] + jnp.arange(a) - starts[se]
    dest = jnp.zeros((a,), jnp.int32).at[order].set(dest_sorted.astype(jnp.int32))
    blk_e = jnp.minimum(jnp.searchsorted(pad_end, jnp.arange(n_blk) * bm, side='right'),
                        n_experts - 1).astype(jnp.int32)
    n_used = (pad_end[-1] // bm).astype(jnp.int32).reshape(1)
    row_id = jnp.arange(n_blk * bm)
    row_e = jnp.repeat(blk_e, bm)
    sorted_pos = jnp.clip(starts[row_e] + row_id - pad_start[row_e], 0, a - 1)
    src_tok = order[sorted_pos] // TOP_K
    rows = jnp.take(h2, src_tok, axis=0)
    out = _expert_blocks(rows, blk_e, n_used, wgu, bgu, wdn, bdn, bm)
    picked = jnp.take(out, dest, axis=0).reshape(n, TOP_K, d)
    return jnp.sum(picked * gates[:, :, None], axis=1)


def _final_kernel(x_ref, m_ref, w_ref, y_ref):
    y_ref[...] = _rms(x_ref[...] + m_ref[...], w_ref[...])


def _final_norm(x1, moe_out, w):
    n, d = x1.shape
    tm = min(TOKEN_TILE, n)
    tok = pl.BlockSpec((tm, d), lambda i: (i, 0))
    return pl.pallas_call(
        _final_kernel, grid=(n // tm,), in_specs=[tok, tok, _full(w.shape)], out_specs=tok,
        out_shape=jax.ShapeDtypeStruct((n, d), F32),
        compiler_params=_params(("arbitrary",)), name="final_norm")(x1, moe_out, w)


def _to_scan_layout(t, b, t_len, heads):
    return t.reshape(b, t_len, heads, RW_HEAD_DIM).transpose(1, 3, 0, 2).reshape(
        t_len, RW_HEAD_DIM, b * heads)


def _lane_param(p, heads):
    return jnp.tile(p.reshape(heads, RW_HEAD_DIM).T, (1, LANES // heads))


def _group(x, shift_prev, wkv0, attend, w, lam_init):
    b, t_len, d = x.shape
    n = b * t_len
    rw, rd, ri, rg, qk, vw = w['dims']
    heads = rw // RW_HEAD_DIM
    xf = x.reshape(n, d)
    if shift_prev is None:
        start_mask = start_vals = None
    else:
        c1 = shift_prev.shape[1]
        start_mask = (jnp.arange(n) % t_len == 0).astype(F32).reshape(n, 1)
        start_vals = jnp.zeros((b, t_len, c1), F32).at[:, 0].set(shift_prev).reshape(n, c1)
    (pr, r, dec, kx, vx, a, g, k, v, qb, kb, vb) = _in_projection(
        xf, t_len, start_mask, start_vals, w['n1'], w['w_main'], w['mu'], w['w0'], w['w2'], w['a0'],
        w['a2'], w['g2'], w['dims'])

    seqs = [_to_scan_layout(s, b, t_len, heads) for s in (r, dec, kx, vx, a, g)]
    if wkv0 is None:
        s0 = jnp.zeros((RW_HEAD_DIM, RW_HEAD_DIM, b * heads), F32)
    else:
        s0 = wkv0.transpose(3, 2, 0, 1).reshape(RW_HEAD_DIM, RW_HEAD_DIM, b * heads)
    ya_t, s_t = _wkv_scan(seqs, s0, w['lane_params'])
    ya = ya_t.reshape(t_len, RW_HEAD_DIM, b, heads).transpose(2, 0, 3, 1).reshape(n, rw).astype(BF16)
    wkv_new = s_t.reshape(RW_HEAD_DIM, RW_HEAD_DIM, b, heads).transpose(2, 3, 1, 0)

    o = attend(qb, kb, vb, k, v)
    x1, h2, route = _merge_and_route(xf, ya, o.reshape(n, vw), w['n1'], w['wg'], w['wr'], w['wd'],
                                     w['wo'], w['n2'], w['rhi'], w['rlo'], w['rb'], w['n_experts'])
    moe_out = _moe(h2, route, w['wgu'], w['bgu'], w['wdn'], w['bdn'], w['n_experts'])
    y = _final_norm(x1, moe_out, w['nf'])
    shift_new = pr.reshape(b, t_len, -1)[:, -1]
    return y.reshape(b, t_len, d), k, v, wkv_new, shift_new


def kernel(x_prompt, x_sample, cache_k, cache_v, state_wkv, state_shift, page_table, meta_tokens, norm1_w, w_in, rw_mu, rw_w0, rw_w2, rw_a0, rw_a2, rw_g2, rw_k_k, rw_k_a, rw_r_k, rw_ln_w, rw_ln_b, w_out_rwkv, da_lq1, da_lk1, da_lq2, da_lk2, da_subln_w, w_out_diff, w_out, norm2_w, w_router, b_router, w_gu, b_gu, w_down, b_down, norm_f_w):
    depth = w_in.shape[0]
    assert depth == 1
    b, seq, d = x_prompt.shape
    db, dseq, _ = x_sample.shape
    rw = rw_w0.shape[1]
    rd, ri, rg = rw_w2.shape[1], rw_a2.shape[1], rw_g2.shape[1]
    n_heads = cache_k.shape[3]
    qk = n_heads * cache_k.shape[4]
    vw = n_heads * cache_v.shape[4]
    dims = (rw, rd, ri, rg, qk, vw)
    c1 = 3 * rw + rd + ri + rg
    c4 = c1 + 2 * qk + vw
    heads = rw // RW_HEAD_DIM
    n_experts = w_router.shape[2]
    l = 0
    lam_init = 0.8 - 0.6 * math.exp(-0.3 * l)
    row = lambda p: p.reshape(1, -1)

    r_pad = jnp.pad(w_router[l], ((0, 0), (0, LANES - n_experts)))
    r_hi = r_pad.astype(BF16)
    w = dict(
        dims=dims, n_experts=n_experts,
        n1=row(norm1_w[l]), w_main=w_in[l][:, :c4].astype(BF16), mu=row(rw_mu[l]),
        w0=row(rw_w0[l]), w2=rw_w2[l].astype(BF16), a0=row(rw_a0[l]), a2=rw_a2[l].astype(BF16),
        g2=rw_g2[l].astype(BF16),
        lane_params=[_lane_param(p[l], heads) for p in (rw_k_k, rw_k_a, rw_r_k, rw_ln_w, rw_ln_b)],
        wg=w_in[l][:, c4:].astype(BF16), wr=w_out_rwkv[l].astype(BF16),
        wd=w_out_diff[l].astype(BF16), wo=w_out[l].astype(BF16), n2=row(norm2_w[l]),
        rhi=r_hi, rlo=(r_pad - r_hi.astype(F32)).astype(BF16),
        rb=jnp.pad(b_router[l], (0, LANES - n_experts)).reshape(1, LANES),
        wgu=w_gu[l].astype(BF16), bgu=b_gu[l].reshape(n_experts, 1, -1),
        wdn=w_down[l].astype(BF16), bdn=b_down[l].reshape(n_experts, 1, -1), nf=row(norm_f_w))
    lam_params = [row(p[l]) for p in (da_lq1, da_lk1, da_lq2, da_lk2)]
    subw = row(da_subln_w[l])

    meta = jnp.broadcast_to(meta_tokens[None], (b, N_META, d))
    xp = jnp.concatenate([meta, x_prompt], axis=1)
    lp = seq + N_META

    def attend_prompt(qb, kb, vb, k, v):
        del k, v
        shp = lambda t: t.reshape(b, lp, -1)
        return _prompt_attention(shp(qb), shp(kb), shp(vb), lam_params, subw, lam_init, n_heads)

    ck = cache_k[l].reshape(cache_k.shape[1], cache_k.shape[2], qk)
    cv = cache_v[l].reshape(cache_v.shape[1], cache_v.shape[2], vw)

    def attend_sample(qb, kb, vb, k, v):
        del kb, vb
        pad = lambda t: jnp.pad(t.reshape(db, dseq, -1), ((0, 0), (0, LANES - dseq), (0, 0)))
        return _sample_attention(qb.astype(F32).reshape(db, dseq, qk), pad(k), pad(v), ck, cv,
                                 page_table, lam_params, subw, lam_init, n_heads)

    yp, kp, vp, wp, sp = _group(xp, None, None, attend_prompt, w, lam_init)
    ys, ks, vs, ws, ss = _group(x_sample, state_shift[l], state_wkv[l], attend_sample, w, lam_init)

    hd = lambda t, bb, tt: t.reshape(1, bb, tt, n_heads, -1)
    return (yp[:, N_META:], ys, hd(kp, b, lp), hd(vp, b, lp), wp[None], sp[None],
            hd(ks, db, dseq), hd(vs, db, dseq), ws[None], ss[None])
```

```python
import functools
import math

import jax
import jax.numpy as jnp
from jax import lax
from jax.experimental import pallas as pl
from jax.experimental.pallas import tpu as pltpu

F32 = jnp.float32
BF16 = jnp.bfloat16

N_META = 16
RW_HEAD_DIM = 64
DA_HEAD_DIM = 64
TOP_K = 4
SWIGLU_LIMIT = 7.0
SWIGLU_ALPHA = 1.702
NORM_EPS = 1e-5
RW_GN_EPS = RW_HEAD_DIM * 1e-5
NEG_INF = -1e30
KK_EPS = 1e-12

LANES = 128
TOKEN_TILE = 256
ATTN_TILE = 256
SCAN_CHUNK = 16
PAGES_PER_STEP = 16
MOE_ROWS = 512
VMEM_LIMIT = 56 * 1024 * 1024


def _params(sem):
    return pltpu.CompilerParams(dimension_semantics=sem, vmem_limit_bytes=VMEM_LIMIT)


def _rms(x, w):
    return x * lax.rsqrt(jnp.mean(x * x, axis=-1, keepdims=True) + NORM_EPS) * w


def _full(shape):
    n = len(shape)
    return pl.BlockSpec(shape, lambda *_: (0,) * n)


def _inproj_kernel(seq_len, has_start, rw, rd, ri, rg, qk, vw, *refs):
    if has_start:
        (x_ref, n1_ref, w_ref, mu_ref, w0_ref, w2_ref, a0_ref, a2_ref, g2_ref, sm_ref, sv_ref,
         pr_ref, r_ref, dec_ref, kx_ref, vx_ref, a_ref, g_ref, k_ref, v_ref, qb_ref, kb_ref, vb_ref,
         last_scr) = refs
    else:
        (x_ref, n1_ref, w_ref, mu_ref, w0_ref, w2_ref, a0_ref, a2_ref, g2_ref,
         pr_ref, r_ref, dec_ref, kx_ref, vx_ref, a_ref, g_ref, k_ref, v_ref, qb_ref, kb_ref, vb_ref,
         last_scr) = refs
    i = pl.program_id(0)
    tm = x_ref.shape[0]
    c1 = 3 * rw + rd + ri + rg

    @pl.when(i == 0)
    def _():
        last_scr[...] = jnp.zeros_like(last_scr)

    h = _rms(x_ref[...], n1_ref[...])
    proj = jnp.dot(h.astype(BF16), w_ref[...], preferred_element_type=F32)
    pr = proj[:, :c1]
    pr_ref[...] = pr
    q = proj[:, c1:c1 + qk]
    k = proj[:, c1 + qk:c1 + 2 * qk]
    v = proj[:, c1 + 2 * qk:c1 + 2 * qk + vw]
    k_ref[...] = k
    v_ref[...] = v
    qb_ref[...] = (q * (DA_HEAD_DIM ** -0.5)).astype(BF16)
    kb_ref[...] = k.astype(BF16)
    vb_ref[...] = v.astype(BF16)

    row = lax.broadcasted_iota(jnp.int32, (tm, 1), 0)
    prev = jnp.where(row == 0, last_scr[...], pltpu.roll(pr, 1, 0))
    if has_start:
        prev = jnp.where(sm_ref[...] > 0.0, sv_ref[...], prev)
    else:
        first = lax.rem(seq_len - lax.rem(i * tm, seq_len), seq_len)
        prev = jnp.where(row == first, 0.0, prev)
    last_scr[...] = pr[tm - 1:tm, :]
    xs = pr + mu_ref[...] * (prev - pr)

    r_ref[...] = xs[:, :rw]
    kx_ref[...] = xs[:, rw:2 * rw]
    vx_ref[...] = xs[:, 2 * rw:3 * rw]
    xw = xs[:, 3 * rw:3 * rw + rd]
    xa = xs[:, 3 * rw + rd:3 * rw + rd + ri]
    xg = xs[:, 3 * rw + rd + ri:c1]
    z = w0_ref[...] + jnp.dot(jnp.tanh(xw).astype(BF16), w2_ref[...], preferred_element_type=F32)
    dec_ref[...] = jnp.exp(-math.exp(-0.5) * jax.nn.sigmoid(z))
    a_ref[...] = jax.nn.sigmoid(
        a0_ref[...] + jnp.dot(xa.astype(BF16), a2_ref[...], preferred_element_type=F32))
    g_ref[...] = jnp.dot(jax.nn.sigmoid(xg).astype(BF16), g2_ref[...], preferred_element_type=F32)


def _in_projection(x, seq_len, start_mask, start_vals, n1, w_main, mu, w0, w2, a0, a2, g2, dims):
    rw, rd, ri, rg, qk, vw = dims
    n, d = x.shape
    tm = min(TOKEN_TILE, n)
    assert n % tm == 0
    c1 = 3 * rw + rd + ri + rg
    has_start = start_mask is not None
    tok = lambda w: pl.BlockSpec((tm, w), lambda i: (i, 0))
    in_specs = [tok(d), _full(n1.shape), _full(w_main.shape), _full(mu.shape), _full(w0.shape),
                _full(w2.shape), _full(a0.shape), _full(a2.shape), _full(g2.shape)]
    args = [x, n1, w_main, mu, w0, w2, a0, a2, g2]
    if has_start:
        in_specs += [tok(1), tok(c1)]
        args += [start_mask, start_vals]
    out_shape = ([jax.ShapeDtypeStruct((n, c1), F32)]
                 + [jax.ShapeDtypeStruct((n, rw), F32)] * 6
                 + [jax.ShapeDtypeStruct((n, qk), F32), jax.ShapeDtypeStruct((n, vw), F32),
                    jax.ShapeDtypeStruct((n, qk), BF16), jax.ShapeDtypeStruct((n, qk), BF16),
                    jax.ShapeDtypeStruct((n, vw), BF16)])
    out_specs = [tok(c1)] + [tok(rw)] * 6 + [tok(qk), tok(vw), tok(qk), tok(qk), tok(vw)]
    return pl.pallas_call(
        functools.partial(_inproj_kernel, seq_len, has_start, rw, rd, ri, rg, qk, vw),
        grid=(n // tm,), in_specs=in_specs, out_specs=out_specs, out_shape=out_shape,
        scratch_shapes=[pltpu.VMEM((1, c1), F32)],
        compiler_params=_params(("arbitrary",)), name="in_projection")(*args)


def _scan_kernel(r_ref, w_ref, kx_ref, v_ref, a_ref, g_ref, s0_ref,
                 kkp_ref, kap_ref, rkp_ref, lnw_ref, lnb_ref,
                 out_ref, s_ref, kk_scr, kka_scr, kp_scr, y_scr):
    t = pl.program_id(1)
    tc, n, _ = r_ref.shape

    @pl.when(t == 0)
    def _():
        s_ref[...] = s0_ref[...]

    kx = kx_ref[...]
    a = a_ref[...]
    kk = kx * kkp_ref[...]
    kk = kk / jnp.maximum(jnp.sqrt(jnp.sum(kk * kk, axis=1, keepdims=True)), KK_EPS)
    kk_scr[...] = kk
    kka_scr[...] = kk * a
    kp_scr[...] = kx * (1.0 + (a - 1.0) * kap_ref[...])

    def step(tt, carry):
        sa = jnp.zeros((n, LANES), F32)
        for j in range(n):
            sa = sa + s_ref[j] * kk_scr[tt, pl.ds(j, 1), :]
        sa = -sa
        vt = v_ref[tt]
        y = jnp.zeros((n, LANES), F32)
        for j in range(n):
            row = pl.ds(j, 1)
            sj = (s_ref[j] * w_ref[tt, row, :] + sa * kka_scr[tt, row, :]
                  + vt * kp_scr[tt, row, :])
            s_ref[j] = sj
            y = y + sj * r_ref[tt, row, :]
        y_scr[tt] = y
        return carry

    lax.fori_loop(0, tc, step, 0)

    y = y_scr[...]
    mu = jnp.mean(y, axis=1, keepdims=True)
    var = jnp.mean(jnp.square(y - mu), axis=1, keepdims=True)
    yn = (y - mu) * lax.rsqrt(var + RW_GN_EPS) * lnw_ref[...] + lnb_ref[...]
    bonus = jnp.sum(r_ref[...] * kp_scr[...] * rkp_ref[...], axis=1, keepdims=True) * v_ref[...]
    out_ref[...] = (yn + bonus) * g_ref[...]


def _wkv_scan(seqs, s0, lane_params):
    t_len, n, bh = seqs[0].shape
    tc = SCAN_CHUNK if t_len % SCAN_CHUNK == 0 else t_len
    assert t_len % tc == 0 and bh % LANES == 0
    seq_spec = pl.BlockSpec((tc, n, LANES), lambda l, t: (t, 0, l))
    st_spec = pl.BlockSpec((n, n, LANES), lambda l, t: (0, 0, l))
    par_spec = pl.BlockSpec((n, LANES), lambda l, t: (0, 0))
    return pl.pallas_call(
        _scan_kernel, grid=(bh // LANES, t_len // tc),
        in_specs=[seq_spec] * 6 + [st_spec] + [par_spec] * 5,
        out_specs=[seq_spec, st_spec],
        out_shape=[jax.ShapeDtypeStruct((t_len, n, bh), F32), jax.ShapeDtypeStruct((n, n, bh), F32)],
        scratch_shapes=[pltpu.VMEM((tc, n, LANES), F32)] * 4,
        compiler_params=_params(("arbitrary", "arbitrary")), name="wkv_scan")(*seqs, s0, *lane_params)


def _lambda(lq1_ref, lk1_ref, lq2_ref, lk2_ref, lam_init):
    s1 = jnp.sum(lq1_ref[...] * lk1_ref[...], axis=-1, keepdims=True)
    s2 = jnp.sum(lq2_ref[...] * lk2_ref[...], axis=-1, keepdims=True)
    return jnp.exp(s1) - jnp.exp(s2) + lam_init


def _prompt_attn_kernel(lam_init, q_ref, k_ref, vt_ref, lq1_ref, lk1_ref, lq2_ref, lk2_ref,
                        subw_ref, o_ref, qbd_scr, m_scr, l_scr, acc_scr):
    qi = pl.program_id(1)
    tq = q_ref.shape[1]
    n_heads, _, tk, dk = k_ref.shape[1:]
    dv = vt_ref.shape[3]
    heads = range(n_heads)

    for h in heads:
        qt = q_ref[0, :, h * dk:(h + 1) * dk].astype(F32).T
        sub = lax.broadcasted_iota(jnp.int32, qt.shape, 0)
        qbd_scr[h, :, 0:tq] = jnp.where(sub < DA_HEAD_DIM, qt, 0.0).astype(BF16)
        qbd_scr[h, :, tq:2 * tq] = jnp.where(sub >= DA_HEAD_DIM, qt, 0.0).astype(BF16)
    m_scr[...] = jnp.full_like(m_scr, NEG_INF)
    l_scr[...] = jnp.zeros_like(l_scr)
    acc_scr[...] = jnp.zeros_like(acc_scr)

    def step(kb, mask):
        ss = [jnp.dot(k_ref[0, h, kb], qbd_scr[h], preferred_element_type=F32) for h in heads]
        if mask is not None:
            ss = [jnp.where(mask, s, NEG_INF) for s in ss]
        m_prev = [m_scr[h] for h in heads]
        m_new = [jnp.maximum(m_prev[h], jnp.max(ss[h], axis=0, keepdims=True)) for h in heads]
        ps = [jnp.exp(ss[h] - m_new[h]) for h in heads]
        alpha = [jnp.exp(m_prev[h] - m_new[h]) for h in heads]
        pv = [jnp.dot(vt_ref[0, h, kb], ps[h].astype(BF16), preferred_element_type=F32)
              for h in heads]
        for h in heads:
            l_scr[h] = alpha[h] * l_scr[h] + jnp.sum(ps[h], axis=0, keepdims=True)
            acc_scr[h] = alpha[h] * acc_scr[h] + pv[h]
            m_scr[h] = m_new[h]

    def body(kb, carry):
        step(kb, None)
        return carry

    lax.fori_loop(0, qi, body, 0)

    key = lax.broadcasted_iota(jnp.int32, (tk, 2 * tq), 0)
    qry = lax.broadcasted_iota(jnp.int32, (tk, 2 * tq), 1)
    step(qi, key <= jnp.where(qry >= tq, qry - tq, qry))

    lam = _lambda(lq1_ref, lk1_ref, lq2_ref, lk2_ref, lam_init)
    for h in heads:
        acc = acc_scr[h]
        l = l_scr[h]
        ot = acc[:, :tq] / l[:, :tq] - lam * (acc[:, tq:] / l[:, tq:])
        ot = ot * lax.rsqrt(jnp.mean(ot * ot, axis=0, keepdims=True) + NORM_EPS)
        o_ref[0, :, h * dv:(h + 1) * dv] = (
            ot.T * subw_ref[...] * (1.0 - lam_init)).astype(o_ref.dtype)


def _prompt_attention(qb, kb, vb, lam_params, subw, lam_init, n_heads):
    b, seq_len, _ = qb.shape
    tq = tk = ATTN_TILE
    dk = kb.shape[-1] // n_heads
    dv = vb.shape[-1] // n_heads
    nq = pl.cdiv(seq_len, tq)
    pad = nq * tk - seq_len
    kp = jnp.pad(kb, ((0, 0), (0, pad), (0, 0))).reshape(b, nq, tk, n_heads, dk).transpose(0, 3, 1, 2, 4)
    vt = jnp.pad(vb, ((0, 0), (0, pad), (0, 0))).reshape(b, nq, tk, n_heads, dv).transpose(0, 3, 1, 4, 2)
    small = [_full(p.shape) for p in lam_params] + [_full(subw.shape)]
    return pl.pallas_call(
        functools.partial(_prompt_attn_kernel, lam_init),
        grid=(b, nq),
        in_specs=[pl.BlockSpec((1, tq, n_heads * dk), lambda bi, qi: (bi, qi, 0)),
                  pl.BlockSpec((1, n_heads, nq, tk, dk), lambda bi, qi: (bi, 0, 0, 0, 0)),
                  pl.BlockSpec((1, n_heads, nq, dv, tk), lambda bi, qi: (bi, 0, 0, 0, 0))] + small,
        out_specs=pl.BlockSpec((1, tq, n_heads * dv), lambda bi, qi: (bi, qi, 0)),
        out_shape=jax.ShapeDtypeStruct((b, seq_len, n_heads * dv), BF16),
        scratch_shapes=[pltpu.VMEM((n_heads, dk, 2 * tq), BF16),
                        pltpu.VMEM((n_heads, 1, 2 * tq), F32),
                        pltpu.VMEM((n_heads, 1, 2 * tq), F32),
                        pltpu.VMEM((n_heads, dv, 2 * tq), F32)],
        compiler_params=_params(("arbitrary",) * 2), name="prompt_attention")(
            qb, kp, vt, *lam_params, subw)


def _sample_attn_kernel(n_heads, lam_init, pt_ref, q_ref, kn_ref, vn_ref, lq1_ref, lk1_ref, lq2_ref,
                        lk2_ref, subw_ref, *refs):
    del pt_ref
    pp = PAGES_PER_STEP
    k_pages, v_pages = refs[:pp], refs[pp:2 * pp]
    o_ref, qp_scr, bias_scr, m_scr, l_scr, acc_scr = refs[2 * pp:]
    j = pl.program_id(1)
    t_new = q_ref.shape[1]
    dk = kn_ref.shape[2]
    dv = vn_ref.shape[2]
    half = n_heads * t_new
    rows = 2 * half
    keys = bias_scr.shape[1]

    def row_head(shape):
        return lax.rem(lax.broadcasted_iota(jnp.int32, shape, 0) // t_new, n_heads)

    @pl.when(j == 0)
    def _():
        q = q_ref[0]
        lane = lax.broadcasted_iota(jnp.int32, (t_new, dk), 1)
        pieces = []
        for mp in range(2):
            keep = (lane >= DA_HEAD_DIM) if mp else (lane < DA_HEAD_DIM)
            for h in range(n_heads):
                pieces.append(jnp.where(keep, q[:, h * dk:(h + 1) * dk], 0.0))
        qp_scr[...] = jnp.concatenate(pieces, axis=0).astype(BF16)
        col = lax.broadcasted_iota(jnp.int32, (rows, keys), 1)
        bias_scr[...] = jnp.where(lax.rem(col, n_heads) == row_head((rows, keys)), 0.0, NEG_INF)
        m_scr[...] = jnp.full_like(m_scr, NEG_INF)
        l_scr[...] = jnp.zeros_like(l_scr)
        acc_scr[...] = jnp.zeros_like(acc_scr)

    def scores(k_blk):
        return lax.dot_general(qp_scr[...], k_blk.astype(BF16), (((1,), (1,)), ((), ())),
                               preferred_element_type=F32)

    def update(ss, vs):
        m_prev = m_scr[...]
        m_new = m_prev
        for s in ss:
            m_new = jnp.maximum(m_new, jnp.max(s, axis=-1, keepdims=True))
        alpha = jnp.exp(m_prev - m_new)
        l_new = alpha * l_scr[...]
        acc = alpha * acc_scr[...]
        for s, v_blk in zip(ss, vs):
            p = jnp.exp(s - m_new)
            l_new = l_new + jnp.sum(p, axis=-1, keepdims=True)
            acc = acc + jnp.dot(p.astype(BF16), v_blk.astype(BF16), preferred_element_type=F32)
        l_scr[...] = l_new
        acc_scr[...] = acc
        m_scr[...] = m_new

    bias = bias_scr[...]
    update([scores(k_pages[p][0]) + bias for p in range(pp)], [v_pages[p][0] for p in range(pp)])

    @pl.when(j == pl.num_programs(1) - 1)
    def _():
        nk = kn_ref.shape[1]
        col = lax.broadcasted_iota(jnp.int32, (rows, nk), 1)
        row = lax.broadcasted_iota(jnp.int32, (rows, nk), 0)
        ok = ((lax.rem(col, n_heads) == row_head((rows, nk)))
              & (col // n_heads <= lax.rem(row, t_new)))
        update([jnp.where(ok, scores(kn_ref[0]), NEG_INF)], [vn_ref[0]])
        lam = _lambda(lq1_ref, lk1_ref, lq2_ref, lk2_ref, lam_init)
        on = acc_scr[...] / l_scr[...]
        for h in range(n_heads):
            o = on[h * t_new:(h + 1) * t_new] - lam * on[half + h * t_new:half + (h + 1) * t_new]
            o_ref[0, :, h * dv:(h + 1) * dv] = _rms(o, subw_ref[...]) * (1.0 - lam_init)


def _sample_attention(q, k_new, v_new, cache_k, cache_v, page_table, lam_params, subw, lam_init,
                      n_heads):
    db, t_new, width = q.shape
    n_pages = page_table.shape[1]
    keys, dk = cache_k.shape[1], cache_k.shape[2]
    dv = cache_v.shape[2]
    pp = PAGES_PER_STEP
    assert n_pages % pp == 0
    rows = 2 * n_heads * t_new
    seq = lambda t: pl.BlockSpec((1,) + t.shape[1:], lambda b, j, pt: (b, 0, 0))
    small = [pl.BlockSpec(p.shape, lambda b, j, pt: (0, 0)) for p in (*lam_params, subw)]

    def page_spec(p, w):
        return pl.BlockSpec((1, keys, w), lambda b, j, pt: (pt[b, j * pp + p], 0, 0))

    grid_spec = pltpu.PrefetchScalarGridSpec(
        num_scalar_prefetch=1, grid=(db, n_pages // pp),
        in_specs=[seq(q), seq(k_new), seq(v_new)] + small
        + [page_spec(p, dk) for p in range(pp)] + [page_spec(p, dv) for p in range(pp)],
        out_specs=pl.BlockSpec((1, t_new, n_heads * dv), lambda b, j, pt: (b, 0, 0)),
        scratch_shapes=[pltpu.VMEM((rows, dk), BF16), pltpu.VMEM((rows, keys), F32),
                        pltpu.VMEM((rows, 1), F32), pltpu.VMEM((rows, 1), F32),
                        pltpu.VMEM((rows, dv), F32)])
    return pl.pallas_call(
        functools.partial(_sample_attn_kernel, n_heads, lam_init), grid_spec=grid_spec,
        out_shape=jax.ShapeDtypeStruct((db, t_new, n_heads * dv), F32),
        compiler_params=_params(("arbitrary", "arbitrary")), name="sample_attention")(
            page_table, q, k_new, v_new, *lam_params, subw, *([cache_k] * pp), *([cache_v] * pp))


def _merge_kernel(n_experts, x_ref, ya_ref, o_ref, n1_ref, wg_ref, wr_ref, wd_ref, wo_ref, n2_ref,
                  rhi_ref, rlo_ref, rb_ref, x1_ref, h2_ref, route_ref):
    x = x_ref[...]
    d = x.shape[1]
    h = _rms(x, n1_ref[...])
    gates = jax.nn.sigmoid(jnp.dot(h.astype(BF16), wg_ref[...], preferred_element_type=F32))
    ya = jnp.dot(ya_ref[...].astype(BF16), wr_ref[...], preferred_element_type=F32)
    od = jnp.dot(o_ref[...].astype(BF16), wd_ref[...], preferred_element_type=F32)
    merged = gates[:, :d] * ya + gates[:, d:] * od
    x1 = x + jnp.dot(merged.astype(BF16), wo_ref[...], preferred_element_type=F32)
    x1_ref[...] = x1
    h2 = _rms(x1, n2_ref[...])
    h2_hi = h2.astype(BF16)
    h2_ref[...] = h2_hi
    h2_lo = (h2 - h2_hi.astype(F32)).astype(BF16)
    logits = (jnp.dot(h2_hi, rhi_ref[...], preferred_element_type=F32)
              + jnp.dot(h2_lo, rhi_ref[...], preferred_element_type=F32)
              + jnp.dot(h2_hi, rlo_ref[...], preferred_element_type=F32) + rb_ref[...])
    lane = lax.broadcasted_iota(jnp.int32, logits.shape, 1)
    work = jnp.where(lane < n_experts, logits, -jnp.inf)
    vals, idxs = [], []
    for _ in range(TOP_K):
        m = jnp.max(work, axis=-1, keepdims=True)
        idx = jnp.min(jnp.where(work == m, lane, LANES), axis=-1, keepdims=True)
        vals.append(m)
        idxs.append(idx)
        work = jnp.where(lane == idx, -jnp.inf, work)
    es = [jnp.exp(v - vals[0]) for v in vals]
    denom = es[0] + es[1] + es[2] + es[3]
    route = jnp.zeros(logits.shape, F32)
    for kk in range(TOP_K):
        route = jnp.where(lane == kk, es[kk] / denom, route)
        route = jnp.where(lane == TOP_K + kk, idxs[kk].astype(F32), route)
    route_ref[...] = route


def _merge_and_route(x, ya, o, n1, wg, wr, wd, wo, n2, rhi, rlo, rb, n_experts):
    n, d = x.shape
    tm = min(TOKEN_TILE, n)
    assert n % tm == 0
    tok = lambda w: pl.BlockSpec((tm, w), lambda i: (i, 0))
    weights = (n1, wg, wr, wd, wo, n2, rhi, rlo, rb)
    return pl.pallas_call(
        functools.partial(_merge_kernel, n_experts), grid=(n // tm,),
        in_specs=[tok(d), tok(ya.shape[1]), tok(o.shape[1])] + [_full(w.shape) for w in weights],
        out_specs=[tok(d), tok(d), tok(LANES)],
        out_shape=[jax.ShapeDtypeStruct((n, d), F32), jax.ShapeDtypeStruct((n, d), BF16),
                   jax.ShapeDtypeStruct((n, LANES), F32)],
        compiler_params=_params(("arbitrary",)), name="merge_and_route")(x, ya, o, *weights)


def _expert_kernel(blk_e_ref, n_used_ref, rows_ref, wgu_ref, bgu_ref, wdn_ref, bdn_ref, out_ref):
    del blk_e_ref
    i = pl.program_id(0)
    f = wdn_ref.shape[1]

    @pl.when(i < n_used_ref[0])
    def _():
        gu = jnp.dot(rows_ref[...], wgu_ref[0], preferred_element_type=F32) + bgu_ref[0]
        gate = jnp.minimum(gu[:, :f], SWIGLU_LIMIT)
        up = jnp.clip(gu[:, f:], -SWIGLU_LIMIT, SWIGLU_LIMIT)
        glu = gate * jax.nn.sigmoid(gate * SWIGLU_ALPHA)
        mid = ((up + 1.0) * glu).astype(BF16)
        out_ref[...] = jnp.dot(mid, wdn_ref[0], preferred_element_type=F32) + bdn_ref[0]

    @pl.when(i >= n_used_ref[0])
    def _():
        out_ref[...] = jnp.zeros_like(out_ref)


def _expert_blocks(rows, blk_e, n_used, wgu, bgu, wdn, bdn, bm):
    n_rows, d = rows.shape
    n_blk = n_rows // bm
    f2, f = wgu.shape[2], wdn.shape[1]
    grid_spec = pltpu.PrefetchScalarGridSpec(
        num_scalar_prefetch=2, grid=(n_blk,),
        in_specs=[pl.BlockSpec((bm, d), lambda i, be, nu: (i, 0)),
                  pl.BlockSpec((1, d, f2), lambda i, be, nu: (be[i], 0, 0)),
                  pl.BlockSpec((1, 1, f2), lambda i, be, nu: (be[i], 0, 0)),
                  pl.BlockSpec((1, f, d), lambda i, be, nu: (be[i], 0, 0)),
                  pl.BlockSpec((1, 1, d), lambda i, be, nu: (be[i], 0, 0))],
        out_specs=pl.BlockSpec((bm, d), lambda i, be, nu: (i, 0)))
    return pl.pallas_call(
        _expert_kernel, grid_spec=grid_spec, out_shape=jax.ShapeDtypeStruct((n_rows, d), F32),
        compiler_params=_params(("arbitrary",)), name="expert_blocks")(
            blk_e, n_used, rows, wgu, bgu, wdn, bdn)


def _moe(h2, route, wgu, bgu, wdn, bdn, n_experts):
    n, d = h2.shape
    a = n * TOP_K
    bm = min(MOE_ROWS, a // n_experts)
    flat_e = route[:, TOP_K:2 * TOP_K].astype(jnp.int32).reshape(-1)
    iota_a = jnp.arange(a, dtype=jnp.int32)
    se, order = lax.sort_key_val(flat_e, iota_a)
    counts = jnp.sum((flat_e[:, None] == jnp.arange(n_experts, dtype=jnp.int32)[None, :])
                     .astype(jnp.int32), axis=0)
    starts = jnp.cumsum(counts) - counts
    padded = (counts + bm - 1) // bm * bm
    pad_end = jnp.cumsum(padded)
    pad_start = pad_end - padded
    n_blk = -(-a // bm) + n_experts
    dest_sorted = (pad_start[se] + iota_a - starts[se]).astype(jnp.int32)
    _, dest = lax.sort_key_val(order, dest_sorted)
    blk_start = jnp.arange(n_blk, dtype=jnp.int32) * bm
    blk_e = jnp.minimum(jnp.sum((pad_end[None, :] <= blk_start[:, None]).astype(jnp.int32), axis=1),
                        n_experts - 1).astype(jnp.int32)
    n_used = (pad_end[-1] // bm).astype(jnp.int32).reshape(1)
    row_id = jnp.arange(n_blk * bm, dtype=jnp.int32)
    row_e = jnp.repeat(blk_e, bm)
    sorted_pos = jnp.clip(starts[row_e] + row_id - pad_start[row_e], 0, a - 1)
    src_tok = order[sorted_pos] // TOP_K
    rows = h2.at[src_tok].get(mode='promise_in_bounds')
    out = _expert_blocks(rows, blk_e, n_used, wgu, bgu, wdn, bdn, bm)
    dest_t = dest.reshape(n, TOP_K).T.reshape(-1)
    return out.at[dest_t].get(mode='promise_in_bounds').reshape(TOP_K, n, d)


def _final_kernel(x_ref, p_ref, route_ref, w_ref, y_ref):
    x = x_ref[...]
    route = route_ref[...]
    for kk in range(TOP_K):
        x = x + route[:, kk:kk + 1] * p_ref[kk]
    y_ref[...] = _rms(x, w_ref[...])


def _final_norm(x1, picked, route, w):
    n, d = x1.shape
    tm = min(TOKEN_TILE, n)
    tok = pl.BlockSpec((tm, d), lambda i: (i, 0))
    return pl.pallas_call(
        _final_kernel, grid=(n // tm,),
        in_specs=[tok, pl.BlockSpec((TOP_K, tm, d), lambda i: (0, i, 0)),
                  pl.BlockSpec((tm, LANES), lambda i: (i, 0)), _full(w.shape)],
        out_specs=tok, out_shape=jax.ShapeDtypeStruct((n, d), F32),
        compiler_params=_params(("arbitrary",)), name="final_norm")(x1, picked, route, w)


def _to_scan_layout(t, b, t_len, heads):
    return t.reshape(b, t_len, heads, RW_HEAD_DIM).transpose(1, 3, 0, 2).reshape(
        t_len, RW_HEAD_DIM, b * heads)


def _lane_param(p, heads):
    return jnp.tile(p.reshape(heads, RW_HEAD_DIM).T, (1, LANES // heads))


def _group(x, shift_prev, wkv0, attend, w, lam_init):
    b, t_len, d = x.shape
    n = b * t_len
    rw, rd, ri, rg, qk, vw = w['dims']
    heads = rw // RW_HEAD_DIM
    xf = x.reshape(n, d)
    if shift_prev is None:
        start_mask = start_vals = None
    else:
        c1 = shift_prev.shape[1]
        start_mask = (jnp.arange(n) % t_len == 0).astype(F32).reshape(n, 1)
        start_vals = jnp.zeros((b, t_len, c1), F32).at[:, 0].set(shift_prev).reshape(n, c1)
    (pr, r, dec, kx, vx, a, g, k, v, qb, kb, vb) = _in_projection(
        xf, t_len, start_mask, start_vals, w['n1'], w['w_main'], w['mu'], w['w0'], w['w2'], w['a0'],
        w['a2'], w['g2'], w['dims'])

    seqs = [_to_scan_layout(s, b, t_len, heads) for s in (r, dec, kx, vx, a, g)]
    if wkv0 is None:
        s0 = jnp.zeros((RW_HEAD_DIM, RW_HEAD_DIM, b * heads), F32)
    else:
        s0 = wkv0.transpose(3, 2, 0, 1).reshape(RW_HEAD_DIM, RW_HEAD_DIM, b * heads)
    ya_t, s_t = _wkv_scan(seqs, s0, w['lane_params'])
    ya = ya_t.reshape(t_len, RW_HEAD_DIM, b, heads).transpose(2, 0, 3, 1).reshape(n, rw).astype(BF16)
    wkv_new = s_t.reshape(RW_HEAD_DIM, RW_HEAD_DIM, b, heads).transpose(2, 3, 1, 0)

    o = attend(qb, kb, vb, k, v)
    x1, h2, route = _merge_and_route(xf, ya, o.reshape(n, vw), w['n1'], w['wg'], w['wr'], w['wd'],
                                     w['wo'], w['n2'], w['rhi'], w['rlo'], w['rb'], w['n_experts'])
    picked = _moe(h2, route, w['wgu'], w['bgu'], w['wdn'], w['bdn'], w['n_experts'])
    y = _final_norm(x1, picked, route, w['nf'])
    shift_new = pr.reshape(b, t_len, -1)[:, -1]
    return y.reshape(b, t_len, d), k, v, wkv_new, shift_new


def kernel(x_prompt, x_sample, cache_k, cache_v, state_wkv, state_shift, page_table, meta_tokens, norm1_w, w_in, rw_mu, rw_w0, rw_w2, rw_a0, rw_a2, rw_g2, rw_k_k, rw_k_a, rw_r_k, rw_ln_w, rw_ln_b, w_out_rwkv, da_lq1, da_lk1, da_lq2, da_lk2, da_subln_w, w_out_diff, w_out, norm2_w, w_router, b_router, w_gu, b_gu, w_down, b_down, norm_f_w):
    depth = w_in.shape[0]
    assert depth == 1
    b, seq, d = x_prompt.shape
    db, dseq, _ = x_sample.shape
    rw = rw_w0.shape[1]
    rd, ri, rg = rw_w2.shape[1], rw_a2.shape[1], rw_g2.shape[1]
    n_pool, page, n_heads, dk = cache_k.shape[1:]
    dv = cache_v.shape[4]
    qk = n_heads * dk
    vw = n_heads * dv
    dims = (rw, rd, ri, rg, qk, vw)
    c1 = 3 * rw + rd + ri + rg
    c4 = c1 + 2 * qk + vw
    heads = rw // RW_HEAD_DIM
    n_experts = w_router.shape[2]
    l = 0
    lam_init = 0.8 - 0.6 * math.exp(-0.3 * l)
    row = lambda p: p.reshape(1, -1)

    r_pad = jnp.pad(w_router[l], ((0, 0), (0, LANES - n_experts)))
    r_hi = r_pad.astype(BF16)
    w = dict(
        dims=dims, n_experts=n_experts,
        n1=row(norm1_w[l]), w_main=w_in[l][:, :c4].astype(BF16), mu=row(rw_mu[l]),
        w0=row(rw_w0[l]), w2=rw_w2[l].astype(BF16), a0=row(rw_a0[l]), a2=rw_a2[l].astype(BF16),
        g2=rw_g2[l].astype(BF16),
        lane_params=[_lane_param(p[l], heads) for p in (rw_k_k, rw_k_a, rw_r_k, rw_ln_w, rw_ln_b)],
        wg=w_in[l][:, c4:].astype(BF16), wr=w_out_rwkv[l].astype(BF16),
        wd=w_out_diff[l].astype(BF16), wo=w_out[l].astype(BF16), n2=row(norm2_w[l]),
        rhi=r_hi, rlo=(r_pad - r_hi.astype(F32)).astype(BF16),
        rb=jnp.pad(b_router[l], (0, LANES - n_experts)).reshape(1, LANES),
        wgu=w_gu[l].astype(BF16), bgu=b_gu[l].reshape(n_experts, 1, -1),
        wdn=w_down[l].astype(BF16), bdn=b_down[l].reshape(n_experts, 1, -1), nf=row(norm_f_w))
    lam_params = [row(p[l]) for p in (da_lq1, da_lk1, da_lq2, da_lk2)]
    subw = row(da_subln_w[l])

    meta = jnp.broadcast_to(meta_tokens[None], (b, N_META, d))
    xp = jnp.concatenate([meta, x_prompt], axis=1)
    lp = seq + N_META

    def attend_prompt(qb, kb, vb, k, v):
        del k, v
        shp = lambda t: t.reshape(b, lp, -1)
        return _prompt_attention(shp(qb), shp(kb), shp(vb), lam_params, subw, lam_init, n_heads)

    ck = cache_k.reshape(n_pool, page * n_heads, dk)
    cv = cache_v.reshape(n_pool, page * n_heads, dv)

    def attend_sample(qb, kb, vb, k, v):
        del kb, vb
        rows = dseq * n_heads
        pad = lambda t, w_: jnp.pad(t.reshape(db, rows, w_), ((0, 0), (0, LANES - rows), (0, 0)))
        return _sample_attention(qb.astype(F32).reshape(db, dseq, qk), pad(k, dk), pad(v, dv), ck, cv,
                                 page_table, lam_params, subw, lam_init, n_heads)

    yp, kp, vp, wp, sp = _group(xp, None, None, attend_prompt, w, lam_init)
    ys, ks, vs, ws, ss = _group(x_sample, state_shift[l], state_wkv[l], attend_sample, w, lam_init)

    hd = lambda t, bb, tt: t.reshape(1, bb, tt, n_heads, -1)
    return (yp[:, N_META:], ys, hd(kp, b, lp), hd(vp, b, lp), wp[None], sp[None],
            hd(ks, db, dseq), hd(vs, db, dseq), ws[None], ss[None])
```

```python
import functools
import math

import jax
import jax.numpy as jnp
from jax import lax
from jax.experimental import pallas as pl
from jax.experimental.pallas import tpu as pltpu

F32 = jnp.float32
BF16 = jnp.bfloat16

N_META = 16
RW_HEAD_DIM = 64
DA_HEAD_DIM = 64
TOP_K = 4
SWIGLU_LIMIT = 7.0
SWIGLU_ALPHA = 1.702
NORM_EPS = 1e-5
RW_GN_EPS = RW_HEAD_DIM * 1e-5
NEG_INF = -1e30
KK_EPS = 1e-12
LOG2E = math.log2(math.e)

LANES = 128
BF16_SUBLANES = 16
TOKEN_TILE = 256
ATTN_TILE = 256
SCAN_CHUNK = 16
PAGES_PER_STEP = 16
MOE_ROWS = 512
VMEM_LIMIT = 56 * 1024 * 1024


def _params(sem):
    return pltpu.CompilerParams(dimension_semantics=sem, vmem_limit_bytes=VMEM_LIMIT)


def _rms(x, w):
    return x * lax.rsqrt(jnp.mean(x * x, axis=-1, keepdims=True) + NORM_EPS) * w


def _full(shape):
    n = len(shape)
    return pl.BlockSpec(shape, lambda *_: (0,) * n)


def _inproj_kernel(seq_len, has_start, rw, rd, ri, rg, qk, vw, *refs):
    if has_start:
        (x_ref, n1_ref, w_ref, mu_ref, w0_ref, w2_ref, a0_ref, a2_ref, g2_ref, sm_ref, sv_ref,
         pr_ref, r_ref, dec_ref, kx_ref, vx_ref, a_ref, g_ref, k_ref, v_ref, qb_ref, kb_ref, vb_ref,
         last_scr) = refs
    else:
        (x_ref, n1_ref, w_ref, mu_ref, w0_ref, w2_ref, a0_ref, a2_ref, g2_ref,
         pr_ref, r_ref, dec_ref, kx_ref, vx_ref, a_ref, g_ref, k_ref, v_ref, qb_ref, kb_ref, vb_ref,
         last_scr) = refs
    i = pl.program_id(0)
    tm = x_ref.shape[0]
    c1 = 3 * rw + rd + ri + rg

    @pl.when(i == 0)
    def _():
        last_scr[...] = jnp.zeros_like(last_scr)

    h = _rms(x_ref[...], n1_ref[...])
    proj = jnp.dot(h.astype(BF16), w_ref[...], preferred_element_type=F32)
    pr = proj[:, :c1]
    pr_ref[...] = pr
    q = proj[:, c1:c1 + qk]
    k = proj[:, c1 + qk:c1 + 2 * qk]
    v = proj[:, c1 + 2 * qk:c1 + 2 * qk + vw]
    n_heads = k_ref.shape[0] // tm
    dk, dv = qk // n_heads, vw // n_heads
    for hh in range(n_heads):
        k_ref[pl.ds(hh, tm, stride=n_heads), :] = k[:, hh * dk:(hh + 1) * dk]
        v_ref[pl.ds(hh, tm, stride=n_heads), :] = v[:, hh * dv:(hh + 1) * dv]
    qb_ref[...] = (q * (DA_HEAD_DIM ** -0.5 * LOG2E)).astype(BF16)
    kb_ref[...] = k.astype(BF16)
    vb_ref[...] = v.astype(BF16)

    row = lax.broadcasted_iota(jnp.int32, (tm, 1), 0)
    prev = jnp.where(row == 0, last_scr[...], pltpu.roll(pr, 1, 0))
    if has_start:
        prev = jnp.where(sm_ref[...] > 0.0, sv_ref[...], prev)
    else:
        first = lax.rem(seq_len - lax.rem(i * tm, seq_len), seq_len)
        prev = jnp.where(row == first, 0.0, prev)
    last_scr[...] = pr[tm - 1:tm, :]
    xs = pr + mu_ref[...] * (prev - pr)

    r_ref[...] = xs[:, :rw]
    kx_ref[...] = xs[:, rw:2 * rw]
    vx_ref[...] = xs[:, 2 * rw:3 * rw]
    xw = xs[:, 3 * rw:3 * rw + rd]
    xa = xs[:, 3 * rw + rd:3 * rw + rd + ri]
    xg = xs[:, 3 * rw + rd + ri:c1]
    z = w0_ref[...] + jnp.dot(jnp.tanh(xw).astype(BF16), w2_ref[...], preferred_element_type=F32)
    dec_ref[...] = jnp.exp(-math.exp(-0.5) * jax.nn.sigmoid(z))
    a_ref[...] = jax.nn.sigmoid(
        a0_ref[...] + jnp.dot(xa.astype(BF16), a2_ref[...], preferred_element_type=F32))
    g_ref[...] = jnp.dot(jax.nn.sigmoid(xg).astype(BF16), g2_ref[...], preferred_element_type=F32)


def _in_projection(x, seq_len, start_mask, start_vals, n1, w_main, mu, w0, w2, a0, a2, g2, dims,
                   n_heads):
    rw, rd, ri, rg, qk, vw = dims
    n, d = x.shape
    tm = min(TOKEN_TILE, n)
    assert n % tm == 0
    c1 = 3 * rw + rd + ri + rg
    has_start = start_mask is not None
    tok = lambda w: pl.BlockSpec((tm, w), lambda i: (i, 0))
    in_specs = [tok(d), _full(n1.shape), _full(w_main.shape), _full(mu.shape), _full(w0.shape),
                _full(w2.shape), _full(a0.shape), _full(a2.shape), _full(g2.shape)]
    args = [x, n1, w_main, mu, w0, w2, a0, a2, g2]
    if has_start:
        in_specs += [tok(1), tok(c1)]
        args += [start_mask, start_vals]
    out_shape = ([jax.ShapeDtypeStruct((n, c1), F32)]
                 + [jax.ShapeDtypeStruct((n, rw), F32)] * 6
                 + [jax.ShapeDtypeStruct((n * n_heads, qk // n_heads), F32),
                    jax.ShapeDtypeStruct((n * n_heads, vw // n_heads), F32),
                    jax.ShapeDtypeStruct((n, qk), BF16), jax.ShapeDtypeStruct((n, qk), BF16),
                    jax.ShapeDtypeStruct((n, vw), BF16)])
    head_rows = lambda w: pl.BlockSpec((tm * n_heads, w // n_heads), lambda i: (i, 0))
    out_specs = ([tok(c1)] + [tok(rw)] * 6
                 + [head_rows(qk), head_rows(vw), tok(qk), tok(qk), tok(vw)])
    return pl.pallas_call(
        functools.partial(_inproj_kernel, seq_len, has_start, rw, rd, ri, rg, qk, vw),
        grid=(n // tm,), in_specs=in_specs, out_specs=out_specs, out_shape=out_shape,
        scratch_shapes=[pltpu.VMEM((1, c1), F32)],
        compiler_params=_params(("arbitrary",)), name="in_projection")(*args)


def _scan_kernel(r_ref, w_ref, kx_ref, v_ref, a_ref, s0_ref,
                 kkp_ref, kap_ref, rkp_ref, lnw_ref, lnb_ref,
                 out_ref, s_ref, kk_scr, kka_scr, kp_scr, y_scr):
    t = pl.program_id(1)
    tc, n, _ = r_ref.shape

    @pl.when(t == 0)
    def _():
        s_ref[...] = s0_ref[...]

    kx = kx_ref[...]
    a = a_ref[...]
    kk = kx * kkp_ref[...]
    kk = kk / jnp.maximum(jnp.sqrt(jnp.sum(kk * kk, axis=1, keepdims=True)), KK_EPS)
    kk_scr[...] = kk
    kka_scr[...] = kk * a
    kp_scr[...] = kx * (1.0 + (a - 1.0) * kap_ref[...])

    def step(tt, carry):
        sa = jnp.zeros((n, LANES), F32)
        for j in range(n):
            sa = sa + s_ref[j] * kk_scr[tt, pl.ds(j, 1), :]
        sa = -sa
        vt = v_ref[tt]
        y = jnp.zeros((n, LANES), F32)
        for j in range(n):
            row = pl.ds(j, 1)
            sj = (s_ref[j] * w_ref[tt, row, :] + sa * kka_scr[tt, row, :]
                  + vt * kp_scr[tt, row, :])
            s_ref[j] = sj
            y = y + sj * r_ref[tt, row, :]
        y_scr[tt] = y
        return carry

    lax.fori_loop(0, tc, step, 0)

    y = y_scr[...]
    mu = jnp.mean(y, axis=1, keepdims=True)
    var = jnp.mean(jnp.square(y - mu), axis=1, keepdims=True)
    yn = (y - mu) * lax.rsqrt(var + RW_GN_EPS) * lnw_ref[...] + lnb_ref[...]
    bonus = jnp.sum(r_ref[...] * kp_scr[...] * rkp_ref[...], axis=1, keepdims=True) * v_ref[...]
    out_ref[...] = yn + bonus


def _wkv_scan(seqs, s0, lane_params):
    t_len, n, bh = seqs[0].shape
    tc = SCAN_CHUNK if t_len % SCAN_CHUNK == 0 else t_len
    assert t_len % tc == 0 and bh % LANES == 0
    seq_spec = pl.BlockSpec((tc, n, LANES), lambda l, t: (t, 0, l))
    st_spec = pl.BlockSpec((n, n, LANES), lambda l, t: (0, 0, l))
    par_spec = pl.BlockSpec((n, LANES), lambda l, t: (0, 0))
    return pl.pallas_call(
        _scan_kernel, grid=(bh // LANES, t_len // tc),
        in_specs=[seq_spec] * 5 + [st_spec] + [par_spec] * 5,
        out_specs=[seq_spec, st_spec],
        out_shape=[jax.ShapeDtypeStruct((t_len, n, bh), F32), jax.ShapeDtypeStruct((n, n, bh), F32)],
        scratch_shapes=[pltpu.VMEM((tc, n, LANES), F32)] * 4,
        compiler_params=_params(("arbitrary", "arbitrary")), name="wkv_scan")(*seqs, s0, *lane_params)


def _lambda(lq1_ref, lk1_ref, lq2_ref, lk2_ref, lam_init):
    s1 = jnp.sum(lq1_ref[...] * lk1_ref[...], axis=-1, keepdims=True)
    s2 = jnp.sum(lq2_ref[...] * lk2_ref[...], axis=-1, keepdims=True)
    return jnp.exp(s1) - jnp.exp(s2) + lam_init


def _prompt_attn_kernel(lam_init, dv, q_ref, k_ref, vt_ref, lq1_ref, lk1_ref, lq2_ref, lk2_ref,
                        subw_ref, o_ref, qbd_scr, m_scr, acc_scr):
    qi = pl.program_id(1)
    tq = q_ref.shape[1]
    n_heads, _, tk, dk = k_ref.shape[1:]
    heads = range(n_heads)

    for h in heads:
        qt = q_ref[0, :, h * dk:(h + 1) * dk].astype(F32).T
        sub = lax.broadcasted_iota(jnp.int32, qt.shape, 0)
        qbd_scr[h, :, 0:tq] = jnp.where(sub < DA_HEAD_DIM, qt, 0.0).astype(BF16)
        qbd_scr[h, :, tq:2 * tq] = jnp.where(sub >= DA_HEAD_DIM, qt, 0.0).astype(BF16)
    m_scr[...] = jnp.full_like(m_scr, NEG_INF)
    acc_scr[...] = jnp.zeros_like(acc_scr)

    def step(kb, mask):
        ss = [jnp.dot(k_ref[0, h, kb], qbd_scr[h], preferred_element_type=F32) for h in heads]
        if mask is not None:
            ss = [jnp.where(mask, s, NEG_INF) for s in ss]
        m_prev = [m_scr[h] for h in heads]
        m_new = [jnp.maximum(m_prev[h], jnp.max(ss[h], axis=0, keepdims=True)) for h in heads]
        ps = [jnp.exp2(ss[h] - m_new[h]).astype(BF16) for h in heads]
        alpha = [jnp.exp2(m_prev[h] - m_new[h]) for h in heads]
        pv = [jnp.dot(vt_ref[0, h, kb], ps[h], preferred_element_type=F32) for h in heads]
        for h in heads:
            acc_scr[h] = alpha[h] * acc_scr[h] + pv[h]
            m_scr[h] = m_new[h]

    def body(kb, carry):
        step(kb, None)
        return carry

    lax.fori_loop(0, qi, body, 0)

    key = lax.broadcasted_iota(jnp.int32, (tk, 2 * tq), 0)
    qry = lax.broadcasted_iota(jnp.int32, (tk, 2 * tq), 1)
    step(qi, key <= jnp.where(qry >= tq, qry - tq, qry))

    lam = _lambda(lq1_ref, lk1_ref, lq2_ref, lk2_ref, lam_init)
    for h in heads:
        acc = acc_scr[h, 0:dv]
        l = acc_scr[h, dv:dv + 1]
        ot = acc[:, :tq] / l[:, :tq] - lam * (acc[:, tq:] / l[:, tq:])
        ot = ot * lax.rsqrt(jnp.mean(ot * ot, axis=0, keepdims=True) + NORM_EPS)
        o_ref[0, :, h * dv:(h + 1) * dv] = (
            ot.T * subw_ref[...] * (1.0 - lam_init)).astype(o_ref.dtype)


def _prompt_attention(qb, kb, vb, lam_params, subw, lam_init, n_heads):
    b, seq_len, _ = qb.shape
    tq = tk = ATTN_TILE
    dk = kb.shape[-1] // n_heads
    dv = vb.shape[-1] // n_heads
    nq = pl.cdiv(seq_len, tq)
    pad = nq * tk - seq_len
    kp = jnp.pad(kb, ((0, 0), (0, pad), (0, 0))).reshape(b, nq, tk, n_heads, dk).transpose(0, 3, 1, 2, 4)
    vt = jnp.pad(vb, ((0, 0), (0, pad), (0, 0))).reshape(b, nq, tk, n_heads, dv).transpose(0, 3, 1, 4, 2)
    ones = jnp.zeros((b, n_heads, nq, BF16_SUBLANES, tk), BF16).at[:, :, :, 0].set(1.0)
    vt = jnp.concatenate([vt, ones], axis=3)
    dvp = dv + BF16_SUBLANES
    small = [_full(p.shape) for p in lam_params] + [_full(subw.shape)]
    return pl.pallas_call(
        functools.partial(_prompt_attn_kernel, lam_init, dv),
        grid=(b, nq),
        in_specs=[pl.BlockSpec((1, tq, n_heads * dk), lambda bi, qi: (bi, qi, 0)),
                  pl.BlockSpec((1, n_heads, nq, tk, dk), lambda bi, qi: (bi, 0, 0, 0, 0)),
                  pl.BlockSpec((1, n_heads, nq, dvp, tk), lambda bi, qi: (bi, 0, 0, 0, 0))] + small,
        out_specs=pl.BlockSpec((1, tq, n_heads * dv), lambda bi, qi: (bi, qi, 0)),
        out_shape=jax.ShapeDtypeStruct((b, seq_len, n_heads * dv), BF16),
        scratch_shapes=[pltpu.VMEM((n_heads, dk, 2 * tq), BF16),
                        pltpu.VMEM((n_heads, 1, 2 * tq), F32),
                        pltpu.VMEM((n_heads, dvp, 2 * tq), F32)],
        compiler_params=_params(("arbitrary",) * 2), name="prompt_attention")(
            qb, kp, vt, *lam_params, subw)


def _sample_attn_kernel(n_heads, lam_init, pt_ref, q_ref, kn_ref, vn_ref, lq1_ref, lk1_ref, lq2_ref,
                        lk2_ref, subw_ref, *refs):
    del pt_ref
    pp = PAGES_PER_STEP
    k_pages, v_pages = refs[:pp], refs[pp:2 * pp]
    o_ref, qp_scr, bias_scr, m_scr, l_scr, acc_scr = refs[2 * pp:]
    j = pl.program_id(1)
    t_new = q_ref.shape[1]
    dk = kn_ref.shape[2]
    dv = vn_ref.shape[2]
    half = n_heads * t_new
    rows = 2 * half
    keys = bias_scr.shape[1]

    def row_head(shape):
        return lax.rem(lax.broadcasted_iota(jnp.int32, shape, 0) // t_new, n_heads)

    @pl.when(j == 0)
    def _():
        q = q_ref[0]
        lane = lax.broadcasted_iota(jnp.int32, (t_new, dk), 1)
        pieces = []
        for mp in range(2):
            keep = (lane >= DA_HEAD_DIM) if mp else (lane < DA_HEAD_DIM)
            for h in range(n_heads):
                pieces.append(jnp.where(keep, q[:, h * dk:(h + 1) * dk], 0.0))
        qp_scr[...] = jnp.concatenate(pieces, axis=0).astype(BF16)
        col = lax.broadcasted_iota(jnp.int32, (rows, keys), 1)
        bias_scr[...] = jnp.where(lax.rem(col, n_heads) == row_head((rows, keys)), 0.0, NEG_INF)
        m_scr[...] = jnp.full_like(m_scr, NEG_INF)
        l_scr[...] = jnp.zeros_like(l_scr)
        acc_scr[...] = jnp.zeros_like(acc_scr)

    def scores(k_blk):
        return lax.dot_general(qp_scr[...], k_blk.astype(BF16), (((1,), (1,)), ((), ())),
                               preferred_element_type=F32)

    def update(ss, vs):
        m_prev = m_scr[...]
        m_new = m_prev
        for s in ss:
            m_new = jnp.maximum(m_new, jnp.max(s, axis=-1, keepdims=True))
        alpha = jnp.exp2(m_prev - m_new)
        l_new = alpha * l_scr[...]
        acc = alpha * acc_scr[...]
        for s, v_blk in zip(ss, vs):
            p = jnp.exp2(s - m_new)
            l_new = l_new + jnp.sum(p, axis=-1, keepdims=True)
            acc = acc + jnp.dot(p.astype(BF16), v_blk.astype(BF16), preferred_element_type=F32)
        l_scr[...] = l_new
        acc_scr[...] = acc
        m_scr[...] = m_new

    bias = bias_scr[...]
    update([scores(k_pages[p][0]) + bias for p in range(pp)], [v_pages[p][0] for p in range(pp)])

    @pl.when(j == pl.num_programs(1) - 1)
    def _():
        nk = kn_ref.shape[1]
        col = lax.broadcasted_iota(jnp.int32, (rows, nk), 1)
        row = lax.broadcasted_iota(jnp.int32, (rows, nk), 0)
        ok = ((lax.rem(col, n_heads) == row_head((rows, nk)))
              & (col // n_heads <= lax.rem(row, t_new)))
        update([jnp.where(ok, scores(kn_ref[0]), NEG_INF)], [vn_ref[0]])
        lam = _lambda(lq1_ref, lk1_ref, lq2_ref, lk2_ref, lam_init)
        on = acc_scr[...] / l_scr[...]
        for h in range(n_heads):
            o = on[h * t_new:(h + 1) * t_new] - lam * on[half + h * t_new:half + (h + 1) * t_new]
            o_ref[0, :, h * dv:(h + 1) * dv] = _rms(o, subw_ref[...]) * (1.0 - lam_init)


def _sample_attention(q, k_new, v_new, cache_k, cache_v, page_table, lam_params, subw, lam_init,
                      n_heads):
    db, t_new, width = q.shape
    n_pages = page_table.shape[1]
    keys, dk = cache_k.shape[1], cache_k.shape[2]
    dv = cache_v.shape[2]
    pp = PAGES_PER_STEP
    assert n_pages % pp == 0
    rows = 2 * n_heads * t_new
    seq = lambda t: pl.BlockSpec((1,) + t.shape[1:], lambda b, j, pt: (b, 0, 0))
    small = [pl.BlockSpec(p.shape, lambda b, j, pt: (0, 0)) for p in (*lam_params, subw)]

    def page_spec(p, w):
        return pl.BlockSpec((1, keys, w), lambda b, j, pt: (pt[b, j * pp + p], 0, 0))

    grid_spec = pltpu.PrefetchScalarGridSpec(
        num_scalar_prefetch=1, grid=(db, n_pages // pp),
        in_specs=[seq(q), seq(k_new), seq(v_new)] + small
        + [page_spec(p, dk) for p in range(pp)] + [page_spec(p, dv) for p in range(pp)],
        out_specs=pl.BlockSpec((1, t_new, n_heads * dv), lambda b, j, pt: (b, 0, 0)),
        scratch_shapes=[pltpu.VMEM((rows, dk), BF16), pltpu.VMEM((rows, keys), F32),
                        pltpu.VMEM((rows, 1), F32), pltpu.VMEM((rows, 1), F32),
                        pltpu.VMEM((rows, dv), F32)])
    return pl.pallas_call(
        functools.partial(_sample_attn_kernel, n_heads, lam_init), grid_spec=grid_spec,
        out_shape=jax.ShapeDtypeStruct((db, t_new, n_heads * dv), F32),
        compiler_params=_params(("arbitrary", "arbitrary")), name="sample_attention")(
            page_table, q, k_new, v_new, *lam_params, subw, *([cache_k] * pp), *([cache_v] * pp))


def _merge_kernel(n_experts, x_ref, ya_ref, g_ref, o_ref, n1_ref, wg_ref, wr_ref, wd_ref, wo_ref,
                  n2_ref, rhi_ref, rlo_ref, rb_ref, x1_ref, h2_ref, route_ref):
    x = x_ref[...]
    d = x.shape[1]
    h = _rms(x, n1_ref[...])
    gates = jax.nn.sigmoid(jnp.dot(h.astype(BF16), wg_ref[...], preferred_element_type=F32))
    ya = jnp.dot((ya_ref[...] * g_ref[...]).astype(BF16), wr_ref[...], preferred_element_type=F32)
    od = jnp.dot(o_ref[...].astype(BF16), wd_ref[...], preferred_element_type=F32)
    merged = gates[:, :d] * ya + gates[:, d:] * od
    x1 = x + jnp.dot(merged.astype(BF16), wo_ref[...], preferred_element_type=F32)
    x1_ref[...] = x1
    h2 = _rms(x1, n2_ref[...])
    h2_hi = h2.astype(BF16)
    h2_ref[...] = h2_hi
    h2_lo = (h2 - h2_hi.astype(F32)).astype(BF16)
    logits = (jnp.dot(h2_hi, rhi_ref[...], preferred_element_type=F32)
              + jnp.dot(h2_lo, rhi_ref[...], preferred_element_type=F32)
              + jnp.dot(h2_hi, rlo_ref[...], preferred_element_type=F32) + rb_ref[...])
    lane = lax.broadcasted_iota(jnp.int32, logits.shape, 1)
    work = jnp.where(lane < n_experts, logits, -jnp.inf)
    vals, idxs = [], []
    for _ in range(TOP_K):
        m = jnp.max(work, axis=-1, keepdims=True)
        idx = jnp.min(jnp.where(work == m, lane, LANES), axis=-1, keepdims=True)
        vals.append(m)
        idxs.append(idx)
        work = jnp.where(lane == idx, -jnp.inf, work)
    es = [jnp.exp(v - vals[0]) for v in vals]
    denom = es[0] + es[1] + es[2] + es[3]
    route = jnp.zeros(logits.shape, F32)
    for kk in range(TOP_K):
        route = jnp.where(lane == kk, es[kk] / denom, route)
        route = jnp.where(lane == TOP_K + kk, idxs[kk].astype(F32), route)
    route_ref[...] = route


def _merge_and_route(x, ya, g, o, n1, wg, wr, wd, wo, n2, rhi, rlo, rb, n_experts):
    n, d = x.shape
    tm = min(TOKEN_TILE, n)
    assert n % tm == 0
    tok = lambda w: pl.BlockSpec((tm, w), lambda i: (i, 0))
    weights = (n1, wg, wr, wd, wo, n2, rhi, rlo, rb)
    return pl.pallas_call(
        functools.partial(_merge_kernel, n_experts), grid=(n // tm,),
        in_specs=[tok(d), tok(ya.shape[1]), tok(g.shape[1]), tok(o.shape[1])]
        + [_full(w.shape) for w in weights],
        out_specs=[tok(d), tok(d), tok(LANES)],
        out_shape=[jax.ShapeDtypeStruct((n, d), F32), jax.ShapeDtypeStruct((n, d), BF16),
                   jax.ShapeDtypeStruct((n, LANES), F32)],
        compiler_params=_params(("arbitrary",)), name="merge_and_route")(x, ya, g, o, *weights)


def _expert_kernel(blk_e_ref, n_used_ref, rows_ref, wgu_ref, bgu_ref, wdn_ref, bdn_ref, out_ref):
    del blk_e_ref
    i = pl.program_id(0)
    f = wdn_ref.shape[1]

    @pl.when(i < n_used_ref[0])
    def _():
        gu = jnp.dot(rows_ref[...], wgu_ref[0], preferred_element_type=F32) + bgu_ref[0]
        gate = jnp.minimum(gu[:, :f], SWIGLU_LIMIT)
        up = jnp.clip(gu[:, f:], -SWIGLU_LIMIT, SWIGLU_LIMIT)
        glu = gate * jax.nn.sigmoid(gate * SWIGLU_ALPHA)
        mid = ((up + 1.0) * glu).astype(BF16)
        out_ref[...] = jnp.dot(mid, wdn_ref[0], preferred_element_type=F32) + bdn_ref[0]

    @pl.when(i >= n_used_ref[0])
    def _():
        out_ref[...] = jnp.zeros_like(out_ref)


def _expert_blocks(rows, blk_e, n_used, wgu, bgu, wdn, bdn, bm):
    n_rows, d = rows.shape
    n_blk = n_rows // bm
    f2, f = wgu.shape[2], wdn.shape[1]
    grid_spec = pltpu.PrefetchScalarGridSpec(
        num_scalar_prefetch=2, grid=(n_blk,),
        in_specs=[pl.BlockSpec((bm, d), lambda i, be, nu: (i, 0)),
                  pl.BlockSpec((1, d, f2), lambda i, be, nu: (be[i], 0, 0)),
                  pl.BlockSpec((1, 1, f2), lambda i, be, nu: (be[i], 0, 0)),
                  pl.BlockSpec((1, f, d), lambda i, be, nu: (be[i], 0, 0)),
                  pl.BlockSpec((1, 1, d), lambda i, be, nu: (be[i], 0, 0))],
        out_specs=pl.BlockSpec((bm, d), lambda i, be, nu: (i, 0)))
    return pl.pallas_call(
        _expert_kernel, grid_spec=grid_spec, out_shape=jax.ShapeDtypeStruct((n_rows, d), F32),
        compiler_params=_params(("arbitrary",)), name="expert_blocks")(
            blk_e, n_used, rows, wgu, bgu, wdn, bdn)


def _moe(h2, route, wgu, bgu, wdn, bdn, n_experts):
    n, d = h2.shape
    a = n * TOP_K
    bm = min(MOE_ROWS, a // n_experts)
    flat_e = route[:, TOP_K:2 * TOP_K].astype(jnp.int32).reshape(-1)
    iota_a = jnp.arange(a, dtype=jnp.int32)
    se, order = lax.sort_key_val(flat_e, iota_a)
    counts = jnp.sum((flat_e[:, None] == jnp.arange(n_experts, dtype=jnp.int32)[None, :])
                     .astype(jnp.int32), axis=0)
    starts = jnp.cumsum(counts) - counts
    padded = (counts + bm - 1) // bm * bm
    pad_end = jnp.cumsum(padded)
    pad_start = pad_end - padded
    n_blk = -(-a // bm) + n_experts
    dest_sorted = (pad_start[se] + iota_a - starts[se]).astype(jnp.int32)
    _, dest = lax.sort_key_val(order, dest_sorted)
    blk_start = jnp.arange(n_blk, dtype=jnp.int32) * bm
    blk_e = jnp.minimum(jnp.sum((pad_end[None, :] <= blk_start[:, None]).astype(jnp.int32), axis=1),
                        n_experts - 1).astype(jnp.int32)
    n_used = (pad_end[-1] // bm).astype(jnp.int32).reshape(1)
    row_id = jnp.arange(n_blk * bm, dtype=jnp.int32)
    row_e = jnp.repeat(blk_e, bm)
    sorted_pos = jnp.clip(starts[row_e] + row_id - pad_start[row_e], 0, a - 1)
    src_tok = order[sorted_pos] // TOP_K
    rows = h2.at[src_tok].get(mode='promise_in_bounds')
    out = _expert_blocks(rows, blk_e, n_used, wgu, bgu, wdn, bdn, bm)
    dest_t = dest.reshape(n, TOP_K).T.reshape(-1)
    return out.at[dest_t].get(mode='promise_in_bounds').reshape(TOP_K, n, d)


def _final_kernel(x_ref, p_ref, route_ref, w_ref, y_ref):
    x = x_ref[...]
    route = route_ref[...]
    for kk in range(TOP_K):
        x = x + route[:, kk:kk + 1] * p_ref[kk]
    y_ref[...] = _rms(x, w_ref[...])


def _final_norm(x1, picked, route, w):
    n, d = x1.shape
    tm = min(TOKEN_TILE, n)
    tok = pl.BlockSpec((tm, d), lambda i: (i, 0))
    return pl.pallas_call(
        _final_kernel, grid=(n // tm,),
        in_specs=[tok, pl.BlockSpec((TOP_K, tm, d), lambda i: (0, i, 0)),
                  pl.BlockSpec((tm, LANES), lambda i: (i, 0)), _full(w.shape)],
        out_specs=tok, out_shape=jax.ShapeDtypeStruct((n, d), F32),
        compiler_params=_params(("arbitrary",)), name="final_norm")(x1, picked, route, w)


def _to_scan_layout(t, b, t_len, heads):
    return t.reshape(b, t_len, heads, RW_HEAD_DIM).transpose(1, 3, 0, 2).reshape(
        t_len, RW_HEAD_DIM, b * heads)


def _lane_param(p, heads):
    return jnp.tile(p.reshape(heads, RW_HEAD_DIM).T, (1, LANES // heads))


def _group(x, shift_prev, wkv0, attend, w, lam_init):
    b, t_len, d = x.shape
    n = b * t_len
    rw, rd, ri, rg, qk, vw = w['dims']
    heads = rw // RW_HEAD_DIM
    xf = x.reshape(n, d)
    if shift_prev is None:
        start_mask = start_vals = None
    else:
        c1 = shift_prev.shape[1]
        start_mask = (jnp.arange(n) % t_len == 0).astype(F32).reshape(n, 1)
        start_vals = jnp.zeros((b, t_len, c1), F32).at[:, 0].set(shift_prev).reshape(n, c1)
    (pr, r, dec, kx, vx, a, g, k, v, qb, kb, vb) = _in_projection(
        xf, t_len, start_mask, start_vals, w['n1'], w['w_main'], w['mu'], w['w0'], w['w2'], w['a0'],
        w['a2'], w['g2'], w['dims'], w['n_heads'])

    seqs = [_to_scan_layout(s, b, t_len, heads) for s in (r, dec, kx, vx, a)]
    if wkv0 is None:
        s0 = jnp.zeros((RW_HEAD_DIM, RW_HEAD_DIM, b * heads), F32)
    else:
        s0 = wkv0.transpose(3, 2, 0, 1).reshape(RW_HEAD_DIM, RW_HEAD_DIM, b * heads)
    ya_t, s_t = _wkv_scan(seqs, s0, w['lane_params'])
    ya = ya_t.reshape(t_len, RW_HEAD_DIM, b, heads).transpose(2, 0, 3, 1).reshape(n, rw)
    wkv_new = s_t.reshape(RW_HEAD_DIM, RW_HEAD_DIM, b, heads).transpose(2, 3, 1, 0)

    o = attend(qb, kb, vb, k, v)
    x1, h2, route = _merge_and_route(xf, ya, g, o.reshape(n, vw), w['n1'], w['wg'], w['wr'], w['wd'],
                                     w['wo'], w['n2'], w['rhi'], w['rlo'], w['rb'], w['n_experts'])
    picked = _moe(h2, route, w['wgu'], w['bgu'], w['wdn'], w['bdn'], w['n_experts'])
    y = _final_norm(x1, picked, route, w['nf'])
    shift_new = pr.reshape(b, t_len, -1)[:, -1]
    return y.reshape(b, t_len, d), k, v, wkv_new, shift_new


def kernel(x_prompt, x_sample, cache_k, cache_v, state_wkv, state_shift, page_table, meta_tokens, norm1_w, w_in, rw_mu, rw_w0, rw_w2, rw_a0, rw_a2, rw_g2, rw_k_k, rw_k_a, rw_r_k, rw_ln_w, rw_ln_b, w_out_rwkv, da_lq1, da_lk1, da_lq2, da_lk2, da_subln_w, w_out_diff, w_out, norm2_w, w_router, b_router, w_gu, b_gu, w_down, b_down, norm_f_w):
    depth = w_in.shape[0]
    assert depth == 1
    b, seq, d = x_prompt.shape
    db, dseq, _ = x_sample.shape
    rw = rw_w0.shape[1]
    rd, ri, rg = rw_w2.shape[1], rw_a2.shape[1], rw_g2.shape[1]
    n_pool, page, n_heads, dk = cache_k.shape[1:]
    dv = cache_v.shape[4]
    qk = n_heads * dk
    vw = n_heads * dv
    dims = (rw, rd, ri, rg, qk, vw)
    c1 = 3 * rw + rd + ri + rg
    c4 = c1 + 2 * qk + vw
    heads = rw // RW_HEAD_DIM
    n_experts = w_router.shape[2]
    l = 0
    lam_init = 0.8 - 0.6 * math.exp(-0.3 * l)
    row = lambda p: p.reshape(1, -1)

    r_pad = jnp.pad(w_router[l], ((0, 0), (0, LANES - n_experts)))
    r_hi = r_pad.astype(BF16)
    w = dict(
        dims=dims, n_experts=n_experts, n_heads=n_heads,
        n1=row(norm1_w[l]), w_main=w_in[l][:, :c4].astype(BF16), mu=row(rw_mu[l]),
        w0=row(rw_w0[l]), w2=rw_w2[l].astype(BF16), a0=row(rw_a0[l]), a2=rw_a2[l].astype(BF16),
        g2=rw_g2[l].astype(BF16),
        lane_params=[_lane_param(p[l], heads) for p in (rw_k_k, rw_k_a, rw_r_k, rw_ln_w, rw_ln_b)],
        wg=w_in[l][:, c4:].astype(BF16), wr=w_out_rwkv[l].astype(BF16),
        wd=w_out_diff[l].astype(BF16), wo=w_out[l].astype(BF16), n2=row(norm2_w[l]),
        rhi=r_hi, rlo=(r_pad - r_hi.astype(F32)).astype(BF16),
        rb=jnp.pad(b_router[l], (0, LANES - n_experts)).reshape(1, LANES),
        wgu=w_gu[l].astype(BF16), bgu=b_gu[l].reshape(n_experts, 1, -1),
        wdn=w_down[l].astype(BF16), bdn=b_down[l].reshape(n_experts, 1, -1), nf=row(norm_f_w))
    lam_params = [row(p[l]) for p in (da_lq1, da_lk1, da_lq2, da_lk2)]
    subw = row(da_subln_w[l])

    meta = jnp.broadcast_to(meta_tokens[None], (b, N_META, d))
    xp = jnp.concatenate([meta, x_prompt], axis=1)
    lp = seq + N_META

    def attend_prompt(qb, kb, vb, k, v):
        del k, v
        shp = lambda t: t.reshape(b, lp, -1)
        return _prompt_attention(shp(qb), shp(kb), shp(vb), lam_params, subw, lam_init, n_heads)

    ck = cache_k.reshape(n_pool, page * n_heads, dk)
    cv = cache_v.reshape(n_pool, page * n_heads, dv)

    def attend_sample(qb, kb, vb, k, v):
        del kb, vb
        rows = dseq * n_heads
        pad = lambda t, w_: jnp.pad(t.reshape(db, rows, w_), ((0, 0), (0, LANES - rows), (0, 0)))
        return _sample_attention(qb.astype(F32).reshape(db, dseq, qk), pad(k, dk), pad(v, dv), ck, cv,
                                 page_table, lam_params, subw, lam_init, n_heads)

    yp, kp, vp, wp, sp = _group(xp, None, None, attend_prompt, w, lam_init)
    ys, ks, vs, ws, ss = _group(x_sample, state_shift[l], state_wkv[l], attend_sample, w, lam_init)

    hd = lambda t, bb, tt: t.reshape(1, bb, tt, n_heads, -1)
    return (yp[:, N_META:], ys, hd(kp, b, lp), hd(vp, b, lp), wp[None], sp[None],
            hd(ks, db, dseq), hd(vs, db, dseq), ws[None], ss[None])
```

```python
import functools
import math

import jax
import jax.numpy as jnp
from jax import lax
from jax.experimental import pallas as pl
from jax.experimental.pallas import tpu as pltpu

F32 = jnp.float32
BF16 = jnp.bfloat16

N_META = 16
RW_HEAD_DIM = 64
DA_HEAD_DIM = 64
TOP_K = 4
SWIGLU_LIMIT = 7.0
SWIGLU_ALPHA = 1.702
NORM_EPS = 1e-5
RW_GN_EPS = RW_HEAD_DIM * 1e-5
NEG_INF = -1e30
KK_EPS = 1e-12
LOG2E = math.log2(math.e)

LANES = 128
BF16_SUBLANES = 16
TOKEN_TILE = 256
ATTN_TILE = 256
SCAN_CHUNK = 16
PAGES_PER_STEP = 16
MOE_ROWS = 512
VMEM_LIMIT = 56 * 1024 * 1024


def _params(sem):
    return pltpu.CompilerParams(dimension_semantics=sem, vmem_limit_bytes=VMEM_LIMIT)


def _rms(x, w):
    return x * lax.rsqrt(jnp.mean(x * x, axis=-1, keepdims=True) + NORM_EPS) * w


def _full(shape):
    n = len(shape)
    return pl.BlockSpec(shape, lambda *_: (0,) * n)


def _inproj_kernel(seq_len, has_start, rw, rd, ri, rg, qk, vw, *refs):
    if has_start:
        (x_ref, n1_ref, w_ref, mu_ref, w0_ref, w2_ref, a0_ref, a2_ref, g2_ref, sm_ref, sv_ref,
         pr_ref, seq_ref, g_ref, k_ref, v_ref, qkv_ref, last_scr) = refs
    else:
        (x_ref, n1_ref, w_ref, mu_ref, w0_ref, w2_ref, a0_ref, a2_ref, g2_ref,
         pr_ref, seq_ref, g_ref, k_ref, v_ref, qkv_ref, last_scr) = refs
    r_ref, dec_ref, kx_ref, vx_ref, a_ref = (seq_ref.at[s] for s in range(5))
    qb_ref, kb_ref, vb_ref = (qkv_ref.at[s] for s in range(3))
    i = pl.program_id(0)
    tm = x_ref.shape[0]
    c1 = 3 * rw + rd + ri + rg

    @pl.when(i == 0)
    def _():
        last_scr[...] = jnp.zeros_like(last_scr)

    h = _rms(x_ref[...], n1_ref[...])
    proj = jnp.dot(h.astype(BF16), w_ref[...], preferred_element_type=F32)
    pr = proj[:, :c1]
    pr_ref[...] = pr
    q = proj[:, c1:c1 + qk]
    k = proj[:, c1 + qk:c1 + 2 * qk]
    v = proj[:, c1 + 2 * qk:c1 + 2 * qk + vw]
    n_heads = k_ref.shape[0] // tm
    dk, dv = qk // n_heads, vw // n_heads
    for hh in range(n_heads):
        k_ref[pl.ds(hh, tm, stride=n_heads), :] = k[:, hh * dk:(hh + 1) * dk]
        v_ref[pl.ds(hh, tm, stride=n_heads), :] = v[:, hh * dv:(hh + 1) * dv]
    qb_ref[...] = (q * (DA_HEAD_DIM ** -0.5 * LOG2E)).astype(BF16)
    kb_ref[...] = k.astype(BF16)
    vb_ref[...] = v.astype(BF16)

    row = lax.broadcasted_iota(jnp.int32, (tm, 1), 0)
    prev = jnp.where(row == 0, last_scr[...], pltpu.roll(pr, 1, 0))
    if has_start:
        prev = jnp.where(sm_ref[...] > 0.0, sv_ref[...], prev)
    else:
        first = lax.rem(seq_len - lax.rem(i * tm, seq_len), seq_len)
        prev = jnp.where(row == first, 0.0, prev)
    last_scr[...] = pr[tm - 1:tm, :]
    xs = pr + mu_ref[...] * (prev - pr)

    r_ref[...] = xs[:, :rw]
    kx_ref[...] = xs[:, rw:2 * rw]
    vx_ref[...] = xs[:, 2 * rw:3 * rw]
    xw = xs[:, 3 * rw:3 * rw + rd]
    xa = xs[:, 3 * rw + rd:3 * rw + rd + ri]
    xg = xs[:, 3 * rw + rd + ri:c1]
    z = w0_ref[...] + jnp.dot(jnp.tanh(xw).astype(BF16), w2_ref[...], preferred_element_type=F32)
    dec_ref[...] = jnp.exp(-math.exp(-0.5) * jax.nn.sigmoid(z))
    a_ref[...] = jax.nn.sigmoid(
        a0_ref[...] + jnp.dot(xa.astype(BF16), a2_ref[...], preferred_element_type=F32))
    g_ref[...] = jnp.dot(jax.nn.sigmoid(xg).astype(BF16), g2_ref[...], preferred_element_type=F32)


def _in_projection(x, seq_len, start_mask, start_vals, n1, w_main, mu, w0, w2, a0, a2, g2, dims,
                   n_heads):
    rw, rd, ri, rg, qk, vw = dims
    n, d = x.shape
    tm = min(TOKEN_TILE, n)
    assert n % tm == 0
    c1 = 3 * rw + rd + ri + rg
    has_start = start_mask is not None
    tok = lambda w: pl.BlockSpec((tm, w), lambda i: (i, 0))
    in_specs = [tok(d), _full(n1.shape), _full(w_main.shape), _full(mu.shape), _full(w0.shape),
                _full(w2.shape), _full(a0.shape), _full(a2.shape), _full(g2.shape)]
    args = [x, n1, w_main, mu, w0, w2, a0, a2, g2]
    if has_start:
        in_specs += [tok(1), tok(c1)]
        args += [start_mask, start_vals]
    assert qk == vw
    out_shape = [jax.ShapeDtypeStruct((n, c1), F32),
                 jax.ShapeDtypeStruct((5, n, rw), F32),
                 jax.ShapeDtypeStruct((n, rw), F32),
                 jax.ShapeDtypeStruct((n * n_heads, qk // n_heads), F32),
                 jax.ShapeDtypeStruct((n * n_heads, vw // n_heads), F32),
                 jax.ShapeDtypeStruct((3, n, qk), BF16)]
    head_rows = lambda w: pl.BlockSpec((tm * n_heads, w // n_heads), lambda i: (i, 0))
    stack = lambda s, w: pl.BlockSpec((s, tm, w), lambda i: (0, i, 0))
    out_specs = [tok(c1), stack(5, rw), tok(rw), head_rows(qk), head_rows(vw), stack(3, qk)]
    return pl.pallas_call(
        functools.partial(_inproj_kernel, seq_len, has_start, rw, rd, ri, rg, qk, vw),
        grid=(n // tm,), in_specs=in_specs, out_specs=out_specs, out_shape=out_shape,
        scratch_shapes=[pltpu.VMEM((1, c1), F32)],
        compiler_params=_params(("arbitrary",)), name="in_projection")(*args)


def _scan_kernel(r_ref, w_ref, kx_ref, v_ref, a_ref, s0_ref,
                 kkp_ref, kap_ref, rkp_ref, lnw_ref, lnb_ref,
                 out_ref, s_ref, kk_scr, kka_scr, kp_scr, y_scr):
    t = pl.program_id(1)
    tc, n, _ = r_ref.shape

    @pl.when(t == 0)
    def _():
        s_ref[...] = s0_ref[...]

    kx = kx_ref[...]
    a = a_ref[...]
    kk = kx * kkp_ref[...]
    kk = kk / jnp.maximum(jnp.sqrt(jnp.sum(kk * kk, axis=1, keepdims=True)), KK_EPS)
    kk_scr[...] = kk
    kka_scr[...] = kk * a
    kp_scr[...] = kx * (1.0 + (a - 1.0) * kap_ref[...])

    def step(tt, carry):
        sa = jnp.zeros((n, LANES), F32)
        for j in range(n):
            sa = sa + s_ref[j] * kk_scr[tt, pl.ds(j, 1), :]
        sa = -sa
        vt = v_ref[tt]
        y = jnp.zeros((n, LANES), F32)
        for j in range(n):
            row = pl.ds(j, 1)
            sj = (s_ref[j] * w_ref[tt, row, :] + sa * kka_scr[tt, row, :]
                  + vt * kp_scr[tt, row, :])
            s_ref[j] = sj
            y = y + sj * r_ref[tt, row, :]
        y_scr[tt] = y
        return carry

    lax.fori_loop(0, tc, step, 0)

    y = y_scr[...]
    mu = jnp.mean(y, axis=1, keepdims=True)
    var = jnp.mean(jnp.square(y - mu), axis=1, keepdims=True)
    yn = (y - mu) * lax.rsqrt(var + RW_GN_EPS) * lnw_ref[...] + lnb_ref[...]
    bonus = jnp.sum(r_ref[...] * kp_scr[...] * rkp_ref[...], axis=1, keepdims=True) * v_ref[...]
    out_ref[...] = yn + bonus


def _wkv_scan(seqs, s0, lane_params):
    t_len, n, bh = seqs[0].shape
    tc = SCAN_CHUNK if t_len % SCAN_CHUNK == 0 else t_len
    assert t_len % tc == 0 and bh % LANES == 0
    seq_spec = pl.BlockSpec((tc, n, LANES), lambda l, t: (t, 0, l))
    st_spec = pl.BlockSpec((n, n, LANES), lambda l, t: (0, 0, l))
    par_spec = pl.BlockSpec((n, LANES), lambda l, t: (0, 0))
    return pl.pallas_call(
        _scan_kernel, grid=(bh // LANES, t_len // tc),
        in_specs=[seq_spec] * 5 + [st_spec] + [par_spec] * 5,
        out_specs=[seq_spec, st_spec],
        out_shape=[jax.ShapeDtypeStruct((t_len, n, bh), F32), jax.ShapeDtypeStruct((n, n, bh), F32)],
        scratch_shapes=[pltpu.VMEM((tc, n, LANES), F32)] * 4,
        compiler_params=_params(("arbitrary", "arbitrary")), name="wkv_scan")(*seqs, s0, *lane_params)


def _lambda(lq1_ref, lk1_ref, lq2_ref, lk2_ref, lam_init):
    s1 = jnp.sum(lq1_ref[...] * lk1_ref[...], axis=-1, keepdims=True)
    s2 = jnp.sum(lq2_ref[...] * lk2_ref[...], axis=-1, keepdims=True)
    return jnp.exp(s1) - jnp.exp(s2) + lam_init


def _prompt_attn_kernel(lam_init, dv, q_ref, k_ref, vt_ref, lq1_ref, lk1_ref, lq2_ref, lk2_ref,
                        subw_ref, o_ref, qbd_scr, m_scr, acc_scr):
    qi = pl.program_id(1)
    tq = q_ref.shape[2]
    n_heads, _, tk, dk = k_ref.shape[1:]
    heads = range(n_heads)

    for h in heads:
        qt = q_ref[0, 0, :, h * dk:(h + 1) * dk].astype(F32).T
        sub = lax.broadcasted_iota(jnp.int32, qt.shape, 0)
        qbd_scr[h, :, 0:tq] = jnp.where(sub < DA_HEAD_DIM, qt, 0.0).astype(BF16)
        qbd_scr[h, :, tq:2 * tq] = jnp.where(sub >= DA_HEAD_DIM, qt, 0.0).astype(BF16)
    m_scr[...] = jnp.full_like(m_scr, NEG_INF)
    acc_scr[...] = jnp.zeros_like(acc_scr)

    def step(kb, mask):
        ss = [jnp.dot(k_ref[0, h, kb], qbd_scr[h], preferred_element_type=F32) for h in heads]
        if mask is not None:
            ss = [jnp.where(mask, s, NEG_INF) for s in ss]
        m_prev = [m_scr[h] for h in heads]
        m_new = [jnp.maximum(m_prev[h], jnp.max(ss[h], axis=0, keepdims=True)) for h in heads]
        ps = [jnp.exp2(ss[h] - m_new[h]).astype(BF16) for h in heads]
        alpha = [jnp.exp2(m_prev[h] - m_new[h]) for h in heads]
        pv = [jnp.dot(vt_ref[0, h, kb], ps[h], preferred_element_type=F32) for h in heads]
        for h in heads:
            acc_scr[h] = alpha[h] * acc_scr[h] + pv[h]
            m_scr[h] = m_new[h]

    def body(kb, carry):
        step(kb, None)
        return carry

    lax.fori_loop(0, qi, body, 0)

    key = lax.broadcasted_iota(jnp.int32, (tk, 2 * tq), 0)
    qry = lax.broadcasted_iota(jnp.int32, (tk, 2 * tq), 1)
    step(qi, key <= jnp.where(qry >= tq, qry - tq, qry))

    lam = _lambda(lq1_ref, lk1_ref, lq2_ref, lk2_ref, lam_init)
    for h in heads:
        acc = acc_scr[h, 0:dv]
        l = acc_scr[h, dv:dv + 1]
        ot = acc[:, :tq] / l[:, :tq] - lam * (acc[:, tq:] / l[:, tq:])
        ot = ot * lax.rsqrt(jnp.mean(ot * ot, axis=0, keepdims=True) + NORM_EPS)
        o_ref[0, :, h * dv:(h + 1) * dv] = (
            ot.T * subw_ref[...] * (1.0 - lam_init)).astype(o_ref.dtype)


def _prompt_attention(qkv, lam_params, subw, lam_init, n_heads):
    kb, vb = qkv[1], qkv[2]
    b, seq_len, _ = kb.shape
    tq = tk = ATTN_TILE
    dk = kb.shape[-1] // n_heads
    dv = vb.shape[-1] // n_heads
    nq = pl.cdiv(seq_len, tq)
    pad = nq * tk - seq_len
    kp = jnp.pad(kb, ((0, 0), (0, pad), (0, 0))).reshape(b, nq, tk, n_heads, dk).transpose(0, 3, 1, 2, 4)
    vt = jnp.pad(vb, ((0, 0), (0, pad), (0, 0))).reshape(b, nq, tk, n_heads, dv).transpose(0, 3, 1, 4, 2)
    ones = jnp.zeros((b, n_heads, nq, BF16_SUBLANES, tk), BF16).at[:, :, :, 0].set(1.0)
    vt = jnp.concatenate([vt, ones], axis=3)
    dvp = dv + BF16_SUBLANES
    small = [_full(p.shape) for p in lam_params] + [_full(subw.shape)]
    return pl.pallas_call(
        functools.partial(_prompt_attn_kernel, lam_init, dv),
        grid=(b, nq),
        in_specs=[pl.BlockSpec((1, 1, tq, n_heads * dk), lambda bi, qi: (0, bi, qi, 0)),
                  pl.BlockSpec((1, n_heads, nq, tk, dk), lambda bi, qi: (bi, 0, 0, 0, 0)),
                  pl.BlockSpec((1, n_heads, nq, dvp, tk), lambda bi, qi: (bi, 0, 0, 0, 0))] + small,
        out_specs=pl.BlockSpec((1, tq, n_heads * dv), lambda bi, qi: (bi, qi, 0)),
        out_shape=jax.ShapeDtypeStruct((b, seq_len, n_heads * dv), BF16),
        scratch_shapes=[pltpu.VMEM((n_heads, dk, 2 * tq), BF16),
                        pltpu.VMEM((n_heads, 1, 2 * tq), F32),
                        pltpu.VMEM((n_heads, dvp, 2 * tq), F32)],
        compiler_params=_params(("arbitrary",) * 2), name="prompt_attention")(
            qkv, kp, vt, *lam_params, subw)


def _sample_attn_kernel(n_heads, lam_init, pt_ref, q_ref, kn_ref, vn_ref, lq1_ref, lk1_ref, lq2_ref,
                        lk2_ref, subw_ref, ck_hbm, cv_hbm, o_ref, kbuf, vbuf, sem, qp_scr, bias_scr,
                        m_scr, l_scr, acc_scr):
    pp = kbuf.shape[1]
    b = pl.program_id(0)
    j = pl.program_id(1)
    nj = pl.num_programs(1)
    step = b * nj + j
    slot = lax.rem(step, 2)

    def fetch(bb, jj, sl):
        for p in range(pp):
            page = pt_ref[bb, jj * pp + p]
            pltpu.make_async_copy(ck_hbm.at[page], kbuf.at[sl, p], sem.at[0, sl]).start()
            pltpu.make_async_copy(cv_hbm.at[page], vbuf.at[sl, p], sem.at[1, sl]).start()

    @pl.when(step == 0)
    def _():
        fetch(0, 0, 0)

    nxt = step + 1

    @pl.when(nxt < pl.num_programs(0) * nj)
    def _():
        fetch(nxt // nj, lax.rem(nxt, nj), 1 - slot)

    pltpu.make_async_copy(kbuf.at[slot], kbuf.at[slot], sem.at[0, slot]).wait()
    pltpu.make_async_copy(vbuf.at[slot], vbuf.at[slot], sem.at[1, slot]).wait()
    k_pages = [kbuf.at[slot, p] for p in range(pp)]
    v_pages = [vbuf.at[slot, p] for p in range(pp)]
    t_new = q_ref.shape[1]
    dk = kn_ref.shape[2]
    dv = vn_ref.shape[2]
    half = n_heads * t_new
    rows = 2 * half
    keys = bias_scr.shape[1]

    def row_head(shape):
        return lax.rem(lax.broadcasted_iota(jnp.int32, shape, 0) // t_new, n_heads)

    @pl.when(j == 0)
    def _():
        q = q_ref[0]
        lane = lax.broadcasted_iota(jnp.int32, (t_new, dk), 1)
        pieces = []
        for mp in range(2):
            keep = (lane >= DA_HEAD_DIM) if mp else (lane < DA_HEAD_DIM)
            for h in range(n_heads):
                pieces.append(jnp.where(keep, q[:, h * dk:(h + 1) * dk], 0.0))
        qp_scr[...] = jnp.concatenate(pieces, axis=0).astype(BF16)
        col = lax.broadcasted_iota(jnp.int32, (rows, keys), 1)
        bias_scr[...] = jnp.where(lax.rem(col, n_heads) == row_head((rows, keys)), 0.0, NEG_INF)
        m_scr[...] = jnp.full_like(m_scr, NEG_INF)
        l_scr[...] = jnp.zeros_like(l_scr)
        acc_scr[...] = jnp.zeros_like(acc_scr)

    def scores(k_blk):
        return lax.dot_general(qp_scr[...], k_blk.astype(BF16), (((1,), (1,)), ((), ())),
                               preferred_element_type=F32)

    def update(ss, vs):
        m_prev = m_scr[...]
        m_new = m_prev
        for s in ss:
            m_new = jnp.maximum(m_new, jnp.max(s, axis=-1, keepdims=True))
        alpha = jnp.exp2(m_prev - m_new)
        l_new = alpha * l_scr[...]
        acc = alpha * acc_scr[...]
        for s, v_blk in zip(ss, vs):
            p = jnp.exp2(s - m_new)
            l_new = l_new + jnp.sum(p, axis=-1, keepdims=True)
            acc = acc + jnp.dot(p.astype(BF16), v_blk.astype(BF16), preferred_element_type=F32)
        l_scr[...] = l_new
        acc_scr[...] = acc
        m_scr[...] = m_new

    bias = bias_scr[...]
    update([scores(k_pages[p][...]) + bias for p in range(pp)],
           [v_pages[p][...] for p in range(pp)])

    @pl.when(j == pl.num_programs(1) - 1)
    def _():
        nk = kn_ref.shape[1]
        col = lax.broadcasted_iota(jnp.int32, (rows, nk), 1)
        row = lax.broadcasted_iota(jnp.int32, (rows, nk), 0)
        ok = ((lax.rem(col, n_heads) == row_head((rows, nk)))
              & (col // n_heads <= lax.rem(row, t_new)))
        update([jnp.where(ok, scores(kn_ref[0]), NEG_INF)], [vn_ref[0]])
        lam = _lambda(lq1_ref, lk1_ref, lq2_ref, lk2_ref, lam_init)
        on = acc_scr[...] / l_scr[...]
        for h in range(n_heads):
            o = on[h * t_new:(h + 1) * t_new] - lam * on[half + h * t_new:half + (h + 1) * t_new]
            o_ref[0, :, h * dv:(h + 1) * dv] = _rms(o, subw_ref[...]) * (1.0 - lam_init)


def _sample_attention(q, k_new, v_new, cache_k, cache_v, page_table, lam_params, subw, lam_init,
                      n_heads):
    db, t_new, width = q.shape
    n_pages = page_table.shape[1]
    keys, dk = cache_k.shape[1], cache_k.shape[2]
    dv = cache_v.shape[2]
    pp = PAGES_PER_STEP
    assert n_pages % pp == 0
    rows = 2 * n_heads * t_new
    seq = lambda t: pl.BlockSpec((1,) + t.shape[1:], lambda b, j, pt: (b, 0, 0))
    small = [pl.BlockSpec(p.shape, lambda b, j, pt: (0, 0)) for p in (*lam_params, subw)]
    hbm = pl.BlockSpec(memory_space=pl.ANY)
    grid_spec = pltpu.PrefetchScalarGridSpec(
        num_scalar_prefetch=1, grid=(db, n_pages // pp),
        in_specs=[seq(q), seq(k_new), seq(v_new)] + small + [hbm, hbm],
        out_specs=pl.BlockSpec((1, t_new, n_heads * dv), lambda b, j, pt: (b, 0, 0)),
        scratch_shapes=[pltpu.VMEM((2, pp, keys, dk), F32), pltpu.VMEM((2, pp, keys, dv), F32),
                        pltpu.SemaphoreType.DMA((2, 2)),
                        pltpu.VMEM((rows, dk), BF16), pltpu.VMEM((rows, keys), F32),
                        pltpu.VMEM((rows, 1), F32), pltpu.VMEM((rows, 1), F32),
                        pltpu.VMEM((rows, dv), F32)])
    return pl.pallas_call(
        functools.partial(_sample_attn_kernel, n_heads, lam_init), grid_spec=grid_spec,
        out_shape=jax.ShapeDtypeStruct((db, t_new, n_heads * dv), F32),
        compiler_params=_params(("arbitrary", "arbitrary")), name="sample_attention")(
            page_table, q, k_new, v_new, *lam_params, subw, cache_k, cache_v)


def _merge_kernel(n_experts, x_ref, ya_ref, g_ref, o_ref, n1_ref, wg_ref, wr_ref, wd_ref, wo_ref,
                  n2_ref, rhi_ref, rlo_ref, rb_ref, x1_ref, h2_ref, route_ref):
    x = x_ref[...]
    d = x.shape[1]
    h = _rms(x, n1_ref[...])
    gates = jax.nn.sigmoid(jnp.dot(h.astype(BF16), wg_ref[...], preferred_element_type=F32))
    ya = jnp.dot((ya_ref[...] * g_ref[...]).astype(BF16), wr_ref[...], preferred_element_type=F32)
    od = jnp.dot(o_ref[...].astype(BF16), wd_ref[...], preferred_element_type=F32)
    merged = gates[:, :d] * ya + gates[:, d:] * od
    x1 = x + jnp.dot(merged.astype(BF16), wo_ref[...], preferred_element_type=F32)
    x1_ref[...] = x1
    h2 = _rms(x1, n2_ref[...])
    h2_hi = h2.astype(BF16)
    h2_ref[...] = h2_hi
    h2_lo = (h2 - h2_hi.astype(F32)).astype(BF16)
    logits = (jnp.dot(h2_hi, rhi_ref[...], preferred_element_type=F32)
              + jnp.dot(h2_lo, rhi_ref[...], preferred_element_type=F32)
              + jnp.dot(h2_hi, rlo_ref[...], preferred_element_type=F32) + rb_ref[...])
    lane = lax.broadcasted_iota(jnp.int32, logits.shape, 1)
    work = jnp.where(lane < n_experts, logits, -jnp.inf)
    vals, idxs = [], []
    for _ in range(TOP_K):
        m = jnp.max(work, axis=-1, keepdims=True)
        idx = jnp.min(jnp.where(work == m, lane, LANES), axis=-1, keepdims=True)
        vals.append(m)
        idxs.append(idx)
        work = jnp.where(lane == idx, -jnp.inf, work)
    es = [jnp.exp(v - vals[0]) for v in vals]
    denom = es[0] + es[1] + es[2] + es[3]
    route = jnp.zeros(logits.shape, F32)
    for kk in range(TOP_K):
        route = jnp.where(lane == kk, es[kk] / denom, route)
        route = jnp.where(lane == TOP_K + kk, idxs[kk].astype(F32), route)
    route_ref[...] = route


def _merge_and_route(x, ya, g, o, n1, wg, wr, wd, wo, n2, rhi, rlo, rb, n_experts):
    n, d = x.shape
    tm = min(TOKEN_TILE, n)
    assert n % tm == 0
    tok = lambda w: pl.BlockSpec((tm, w), lambda i: (i, 0))
    weights = (n1, wg, wr, wd, wo, n2, rhi, rlo, rb)
    return pl.pallas_call(
        functools.partial(_merge_kernel, n_experts), grid=(n // tm,),
        in_specs=[tok(d), tok(ya.shape[1]), tok(g.shape[1]), tok(o.shape[1])]
        + [_full(w.shape) for w in weights],
        out_specs=[tok(d), tok(d), tok(LANES)],
        out_shape=[jax.ShapeDtypeStruct((n, d), F32), jax.ShapeDtypeStruct((n, d), BF16),
                   jax.ShapeDtypeStruct((n, LANES), F32)],
        compiler_params=_params(("arbitrary",)), name="merge_and_route")(x, ya, g, o, *weights)


def _expert_kernel(blk_e_ref, n_used_ref, rows_ref, wgu_ref, bgu_ref, wdn_ref, bdn_ref, out_ref):
    del blk_e_ref
    i = pl.program_id(0)
    f = wdn_ref.shape[1]

    @pl.when(i < n_used_ref[0])
    def _():
        gu = jnp.dot(rows_ref[...], wgu_ref[0], preferred_element_type=F32) + bgu_ref[0]
        gate = jnp.minimum(gu[:, :f], SWIGLU_LIMIT)
        up = jnp.clip(gu[:, f:], -SWIGLU_LIMIT, SWIGLU_LIMIT)
        glu = gate * jax.nn.sigmoid(gate * SWIGLU_ALPHA)
        mid = ((up + 1.0) * glu).astype(BF16)
        out_ref[...] = jnp.dot(mid, wdn_ref[0], preferred_element_type=F32) + bdn_ref[0]

    @pl.when(i >= n_used_ref[0])
    def _():
        out_ref[...] = jnp.zeros_like(out_ref)


def _expert_blocks(rows, blk_e, n_used, wgu, bgu, wdn, bdn, bm):
    n_rows, d = rows.shape
    n_blk = n_rows // bm
    f2, f = wgu.shape[2], wdn.shape[1]
    grid_spec = pltpu.PrefetchScalarGridSpec(
        num_scalar_prefetch=2, grid=(n_blk,),
        in_specs=[pl.BlockSpec((bm, d), lambda i, be, nu: (i, 0)),
                  pl.BlockSpec((1, d, f2), lambda i, be, nu: (be[i], 0, 0)),
                  pl.BlockSpec((1, 1, f2), lambda i, be, nu: (be[i], 0, 0)),
                  pl.BlockSpec((1, f, d), lambda i, be, nu: (be[i], 0, 0)),
                  pl.BlockSpec((1, 1, d), lambda i, be, nu: (be[i], 0, 0))],
        out_specs=pl.BlockSpec((bm, d), lambda i, be, nu: (i, 0)))
    return pl.pallas_call(
        _expert_kernel, grid_spec=grid_spec, out_shape=jax.ShapeDtypeStruct((n_rows, d), F32),
        compiler_params=_params(("arbitrary",)), name="expert_blocks")(
            blk_e, n_used, rows, wgu, bgu, wdn, bdn)


def _moe(h2, route, wgu, bgu, wdn, bdn, n_experts):
    n, d = h2.shape
    a = n * TOP_K
    bm = min(MOE_ROWS, a // n_experts)
    flat_e = route[:, TOP_K:2 * TOP_K].astype(jnp.int32).reshape(-1)
    iota_a = jnp.arange(a, dtype=jnp.int32)
    se, order = lax.sort_key_val(flat_e, iota_a)
    counts = jnp.sum((flat_e[:, None] == jnp.arange(n_experts, dtype=jnp.int32)[None, :])
                     .astype(jnp.int32), axis=0)
    starts = jnp.cumsum(counts) - counts
    padded = (counts + bm - 1) // bm * bm
    pad_end = jnp.cumsum(padded)
    pad_start = pad_end - padded
    n_blk = -(-a // bm) + n_experts
    dest_sorted = (pad_start[se] + iota_a - starts[se]).astype(jnp.int32)
    _, dest = lax.sort_key_val(order, dest_sorted)
    blk_start = jnp.arange(n_blk, dtype=jnp.int32) * bm
    blk_e = jnp.minimum(jnp.sum((pad_end[None, :] <= blk_start[:, None]).astype(jnp.int32), axis=1),
                        n_experts - 1).astype(jnp.int32)
    n_used = (pad_end[-1] // bm).astype(jnp.int32).reshape(1)
    row_id = jnp.arange(n_blk * bm, dtype=jnp.int32)
    row_e = jnp.repeat(blk_e, bm)
    sorted_pos = jnp.clip(starts[row_e] + row_id - pad_start[row_e], 0, a - 1)
    src_tok = order[sorted_pos] // TOP_K
    rows = h2.at[src_tok].get(mode='promise_in_bounds')
    out = _expert_blocks(rows, blk_e, n_used, wgu, bgu, wdn, bdn, bm)
    dest_t = dest.reshape(n, TOP_K).T.reshape(-1)
    return out.at[dest_t].get(mode='promise_in_bounds').reshape(TOP_K, n, d)


def _final_kernel(x_ref, p_ref, route_ref, w_ref, y_ref):
    x = x_ref[...]
    route = route_ref[...]
    for kk in range(TOP_K):
        x = x + route[:, kk:kk + 1] * p_ref[kk]
    y_ref[...] = _rms(x, w_ref[...])


def _final_norm(x1, picked, route, w):
    n, d = x1.shape
    tm = min(TOKEN_TILE, n)
    tok = pl.BlockSpec((tm, d), lambda i: (i, 0))
    return pl.pallas_call(
        _final_kernel, grid=(n // tm,),
        in_specs=[tok, pl.BlockSpec((TOP_K, tm, d), lambda i: (0, i, 0)),
                  pl.BlockSpec((tm, LANES), lambda i: (i, 0)), _full(w.shape)],
        out_specs=tok, out_shape=jax.ShapeDtypeStruct((n, d), F32),
        compiler_params=_params(("arbitrary",)), name="final_norm")(x1, picked, route, w)


def _to_scan_layout(t, b, t_len, heads):
    return t.reshape(b, t_len, heads, RW_HEAD_DIM).transpose(1, 3, 0, 2).reshape(
        t_len, RW_HEAD_DIM, b * heads)


def _lane_param(p, heads):
    return jnp.tile(p.reshape(heads, RW_HEAD_DIM).T, (1, LANES // heads))


def _group(x, shift_prev, wkv0, attend, w, lam_init):
    b, t_len, d = x.shape
    n = b * t_len
    rw, rd, ri, rg, qk, vw = w['dims']
    heads = rw // RW_HEAD_DIM
    xf = x.reshape(n, d)
    if shift_prev is None:
        start_mask = start_vals = None
    else:
        c1 = shift_prev.shape[1]
        start_mask = (jnp.arange(n) % t_len == 0).astype(F32).reshape(n, 1)
        start_vals = jnp.zeros((b, t_len, c1), F32).at[:, 0].set(shift_prev).reshape(n, c1)
    pr, seq5, g, k, v, qkv = _in_projection(
        xf, t_len, start_mask, start_vals, w['n1'], w['w_main'], w['mu'], w['w0'], w['w2'], w['a0'],
        w['a2'], w['g2'], w['dims'], w['n_heads'])

    seqs = [_to_scan_layout(seq5[s], b, t_len, heads) for s in range(5)]
    if wkv0 is None:
        s0 = jnp.zeros((RW_HEAD_DIM, RW_HEAD_DIM, b * heads), F32)
    else:
        s0 = wkv0.transpose(3, 2, 0, 1).reshape(RW_HEAD_DIM, RW_HEAD_DIM, b * heads)
    ya_t, s_t = _wkv_scan(seqs, s0, w['lane_params'])
    ya = ya_t.reshape(t_len, RW_HEAD_DIM, b, heads).transpose(2, 0, 3, 1).reshape(n, rw)
    wkv_new = s_t.reshape(RW_HEAD_DIM, RW_HEAD_DIM, b, heads).transpose(2, 3, 1, 0)

    o = attend(qkv, k, v)
    x1, h2, route = _merge_and_route(xf, ya, g, o.reshape(n, vw), w['n1'], w['wg'], w['wr'], w['wd'],
                                     w['wo'], w['n2'], w['rhi'], w['rlo'], w['rb'], w['n_experts'])
    picked = _moe(h2, route, w['wgu'], w['bgu'], w['wdn'], w['bdn'], w['n_experts'])
    y = _final_norm(x1, picked, route, w['nf'])
    shift_new = pr.reshape(b, t_len, -1)[:, -1]
    return y.reshape(b, t_len, d), k, v, wkv_new, shift_new


def kernel(x_prompt, x_sample, cache_k, cache_v, state_wkv, state_shift, page_table, meta_tokens, norm1_w, w_in, rw_mu, rw_w0, rw_w2, rw_a0, rw_a2, rw_g2, rw_k_k, rw_k_a, rw_r_k, rw_ln_w, rw_ln_b, w_out_rwkv, da_lq1, da_lk1, da_lq2, da_lk2, da_subln_w, w_out_diff, w_out, norm2_w, w_router, b_router, w_gu, b_gu, w_down, b_down, norm_f_w):
    depth = w_in.shape[0]
    assert depth == 1
    b, seq, d = x_prompt.shape
    db, dseq, _ = x_sample.shape
    rw = rw_w0.shape[1]
    rd, ri, rg = rw_w2.shape[1], rw_a2.shape[1], rw_g2.shape[1]
    n_pool, page, n_heads, dk = cache_k.shape[1:]
    dv = cache_v.shape[4]
    qk = n_heads * dk
    vw = n_heads * dv
    dims = (rw, rd, ri, rg, qk, vw)
    c1 = 3 * rw + rd + ri + rg
    c4 = c1 + 2 * qk + vw
    heads = rw // RW_HEAD_DIM
    n_experts = w_router.shape[2]
    l = 0
    lam_init = 0.8 - 0.6 * math.exp(-0.3 * l)
    row = lambda p: p.reshape(1, -1)

    r_pad = jnp.pad(w_router[l], ((0, 0), (0, LANES - n_experts)))
    r_hi = r_pad.astype(BF16)
    w = dict(
        dims=dims, n_experts=n_experts, n_heads=n_heads,
        n1=row(norm1_w[l]), w_main=w_in[l][:, :c4].astype(BF16), mu=row(rw_mu[l]),
        w0=row(rw_w0[l]), w2=rw_w2[l].astype(BF16), a0=row(rw_a0[l]), a2=rw_a2[l].astype(BF16),
        g2=rw_g2[l].astype(BF16),
        lane_params=[_lane_param(p[l], heads) for p in (rw_k_k, rw_k_a, rw_r_k, rw_ln_w, rw_ln_b)],
        wg=w_in[l][:, c4:].astype(BF16), wr=w_out_rwkv[l].astype(BF16),
        wd=w_out_diff[l].astype(BF16), wo=w_out[l].astype(BF16), n2=row(norm2_w[l]),
        rhi=r_hi, rlo=(r_pad - r_hi.astype(F32)).astype(BF16),
        rb=jnp.pad(b_router[l], (0, LANES - n_experts)).reshape(1, LANES),
        wgu=w_gu[l].astype(BF16), bgu=b_gu[l].reshape(n_experts, 1, -1),
        wdn=w_down[l].astype(BF16), bdn=b_down[l].reshape(n_experts, 1, -1), nf=row(norm_f_w))
    lam_params = [row(p[l]) for p in (da_lq1, da_lk1, da_lq2, da_lk2)]
    subw = row(da_subln_w[l])

    meta = jnp.broadcast_to(meta_tokens[None], (b, N_META, d))
    xp = jnp.concatenate([meta, x_prompt], axis=1)
    lp = seq + N_META

    def attend_prompt(qkv, k, v):
        del k, v
        return _prompt_attention(qkv.reshape(3, b, lp, qk), lam_params, subw, lam_init, n_heads)

    ck = cache_k.reshape(n_pool, page * n_heads, dk)
    cv = cache_v.reshape(n_pool, page * n_heads, dv)

    def attend_sample(qkv, k, v):
        rows = dseq * n_heads
        pad = lambda t, w_: jnp.pad(t.reshape(db, rows, w_), ((0, 0), (0, LANES - rows), (0, 0)))
        return _sample_attention(qkv[0].astype(F32).reshape(db, dseq, qk), pad(k, dk), pad(v, dv),
                                 ck, cv, page_table, lam_params, subw, lam_init, n_heads)

    yp, kp, vp, wp, sp = _group(xp, None, None, attend_prompt, w, lam_init)
    ys, ks, vs, ws, ss = _group(x_sample, state_shift[l], state_wkv[l], attend_sample, w, lam_init)

    hd = lambda t, bb, tt: t.reshape(1, bb, tt, n_heads, -1)
    return (yp[:, N_META:], ys, hd(kp, b, lp), hd(vp, b, lp), wp[None], sp[None],
            hd(ks, db, dseq), hd(vs, db, dseq), ws[None], ss[None])
```

```python
import functools
import math

import jax
import jax.numpy as jnp
from jax import lax
from jax.experimental import pallas as pl
from jax.experimental.pallas import tpu as pltpu

F32 = jnp.float32
BF16 = jnp.bfloat16

N_META = 16
RW_HEAD_DIM = 64
DA_HEAD_DIM = 64
TOP_K = 4
SWIGLU_LIMIT = 7.0
SWIGLU_ALPHA = 1.702
NORM_EPS = 1e-5
RW_GN_EPS = RW_HEAD_DIM * 1e-5
NEG_INF = -1e30
KK_EPS = 1e-12
LOG2E = math.log2(math.e)

LANES = 128
BF16_SUBLANES = 16
TOKEN_TILE = 256
ATTN_TILE = 256
SCAN_CHUNK = 16
PAGES_PER_STEP = 16
MOE_ROWS = 512
VMEM_LIMIT = 56 * 1024 * 1024


def _params(sem):
    return pltpu.CompilerParams(dimension_semantics=sem, vmem_limit_bytes=VMEM_LIMIT)


def _rms(x, w):
    return x * lax.rsqrt(jnp.mean(x * x, axis=-1, keepdims=True) + NORM_EPS) * w


def _full(shape):
    n = len(shape)
    return pl.BlockSpec(shape, lambda *_: (0,) * n)


def _inproj_kernel(seq_len, has_start, rw, rd, ri, rg, qk, vw, *refs):
    if has_start:
        (x_ref, n1_ref, w_ref, mu_ref, w0_ref, w2_ref, a0_ref, a2_ref, g2_ref, sm_ref, sv_ref,
         pr_ref, r_ref, dec_ref, kx_ref, vx_ref, a_ref, g_ref, k_ref, v_ref, qb_ref, kb_ref, vb_ref,
         last_scr) = refs
    else:
        (x_ref, n1_ref, w_ref, mu_ref, w0_ref, w2_ref, a0_ref, a2_ref, g2_ref,
         pr_ref, r_ref, dec_ref, kx_ref, vx_ref, a_ref, g_ref, k_ref, v_ref, qb_ref, kb_ref, vb_ref,
         last_scr) = refs
    i = pl.program_id(0)
    tm = x_ref.shape[0]
    c1 = 3 * rw + rd + ri + rg

    @pl.when(i == 0)
    def _():
        last_scr[...] = jnp.zeros_like(last_scr)

    h = _rms(x_ref[...], n1_ref[...])
    proj = jnp.dot(h.astype(BF16), w_ref[...], preferred_element_type=F32)
    pr = proj[:, :c1]
    pr_ref[...] = pr
    q = proj[:, c1:c1 + qk]
    k = proj[:, c1 + qk:c1 + 2 * qk]
    v = proj[:, c1 + 2 * qk:c1 + 2 * qk + vw]
    n_heads = k_ref.shape[0] // tm
    dk, dv = qk // n_heads, vw // n_heads
    for hh in range(n_heads):
        k_ref[pl.ds(hh, tm, stride=n_heads), :] = k[:, hh * dk:(hh + 1) * dk]
        v_ref[pl.ds(hh, tm, stride=n_heads), :] = v[:, hh * dv:(hh + 1) * dv]
    qb_ref[...] = (q * (DA_HEAD_DIM ** -0.5 * LOG2E)).astype(BF16)
    kb_ref[...] = k.astype(BF16)
    vb_ref[...] = v.astype(BF16)

    row = lax.broadcasted_iota(jnp.int32, (tm, 1), 0)
    prev = jnp.where(row == 0, last_scr[...], pltpu.roll(pr, 1, 0))
    if has_start:
        prev = jnp.where(sm_ref[...] > 0.0, sv_ref[...], prev)
    else:
        first = lax.rem(seq_len - lax.rem(i * tm, seq_len), seq_len)
        prev = jnp.where(row == first, 0.0, prev)
    last_scr[...] = pr[tm - 1:tm, :]
    xs = pr + mu_ref[...] * (prev - pr)

    r_ref[...] = xs[:, :rw]
    kx_ref[...] = xs[:, rw:2 * rw]
    vx_ref[...] = xs[:, 2 * rw:3 * rw]
    xw = xs[:, 3 * rw:3 * rw + rd]
    xa = xs[:, 3 * rw + rd:3 * rw + rd + ri]
    xg = xs[:, 3 * rw + rd + ri:c1]
    z = w0_ref[...] + jnp.dot(jnp.tanh(xw).astype(BF16), w2_ref[...], preferred_element_type=F32)
    dec_ref[...] = jnp.exp(-math.exp(-0.5) * jax.nn.sigmoid(z))
    a_ref[...] = jax.nn.sigmoid(
        a0_ref[...] + jnp.dot(xa.astype(BF16), a2_ref[...], preferred_element_type=F32))
    g_ref[...] = jnp.dot(jax.nn.sigmoid(xg).astype(BF16), g2_ref[...], preferred_element_type=F32)


def _in_projection(x, seq_len, start_mask, start_vals, n1, w_main, mu, w0, w2, a0, a2, g2, dims,
                   n_heads):
    rw, rd, ri, rg, qk, vw = dims
    n, d = x.shape
    tm = min(TOKEN_TILE, n)
    assert n % tm == 0
    c1 = 3 * rw + rd + ri + rg
    has_start = start_mask is not None
    tok = lambda w: pl.BlockSpec((tm, w), lambda i: (i, 0))
    in_specs = [tok(d), _full(n1.shape), _full(w_main.shape), _full(mu.shape), _full(w0.shape),
                _full(w2.shape), _full(a0.shape), _full(a2.shape), _full(g2.shape)]
    args = [x, n1, w_main, mu, w0, w2, a0, a2, g2]
    if has_start:
        in_specs += [tok(1), tok(c1)]
        args += [start_mask, start_vals]
    out_shape = ([jax.ShapeDtypeStruct((n, c1), F32)]
                 + [jax.ShapeDtypeStruct((n, rw), F32)] * 6
                 + [jax.ShapeDtypeStruct((n * n_heads, qk // n_heads), F32),
                    jax.ShapeDtypeStruct((n * n_heads, vw // n_heads), F32),
                    jax.ShapeDtypeStruct((n, qk), BF16), jax.ShapeDtypeStruct((n, qk), BF16),
                    jax.ShapeDtypeStruct((n, vw), BF16)])
    head_rows = lambda w: pl.BlockSpec((tm * n_heads, w // n_heads), lambda i: (i, 0))
    out_specs = ([tok(c1)] + [tok(rw)] * 6
                 + [head_rows(qk), head_rows(vw), tok(qk), tok(qk), tok(vw)])
    return pl.pallas_call(
        functools.partial(_inproj_kernel, seq_len, has_start, rw, rd, ri, rg, qk, vw),
        grid=(n // tm,), in_specs=in_specs, out_specs=out_specs, out_shape=out_shape,
        scratch_shapes=[pltpu.VMEM((1, c1), F32)],
        compiler_params=_params(("arbitrary",)), name="in_projection")(*args)


def _scan_kernel(r_ref, w_ref, kx_ref, v_ref, a_ref, s0_ref,
                 kkp_ref, kap_ref, rkp_ref, lnw_ref, lnb_ref,
                 out_ref, s_ref, kk_scr, kka_scr, kp_scr, rp_scr, y_scr):
    t = pl.program_id(1)
    tc, n, _ = r_ref.shape

    @pl.when(t == 0)
    def _():
        s_ref[...] = s0_ref[...]

    kx = kx_ref[...]
    a = a_ref[...]
    kk = kx * kkp_ref[...]
    kk = kk / jnp.maximum(jnp.sqrt(jnp.sum(kk * kk, axis=1, keepdims=True)), KK_EPS)
    kp = kx * (1.0 + (a - 1.0) * kap_ref[...])
    p = jnp.ones((n, LANES), F32)
    for tt in range(tc):
        kk_scr[tt] = kk[tt] * p
        p = p * w_ref[tt]
        inv = 1.0 / p
        kka_scr[tt] = kk[tt] * a[tt] * inv
        kp_scr[tt] = kp[tt] * inv
        rp_scr[tt] = r_ref[tt] * p

    def step(tt, carry):
        sa = jnp.zeros((n, LANES), F32)
        for j in range(n):
            sa = sa + s_ref[j] * kk_scr[tt, pl.ds(j, 1), :]
        sa = -sa
        vt = v_ref[tt]
        y = jnp.zeros((n, LANES), F32)
        for j in range(n):
            row = pl.ds(j, 1)
            sj = s_ref[j] + sa * kka_scr[tt, row, :] + vt * kp_scr[tt, row, :]
            s_ref[j] = sj
            y = y + sj * rp_scr[tt, row, :]
        y_scr[tt] = y
        return carry

    lax.fori_loop(0, tc, step, 0)
    for j in range(n):
        s_ref[j] = s_ref[j] * p[j:j + 1, :]

    y = y_scr[...]
    mu = jnp.mean(y, axis=1, keepdims=True)
    var = jnp.mean(jnp.square(y - mu), axis=1, keepdims=True)
    yn = (y - mu) * lax.rsqrt(var + RW_GN_EPS) * lnw_ref[...] + lnb_ref[...]
    bonus = jnp.sum(r_ref[...] * kp * rkp_ref[...], axis=1, keepdims=True) * v_ref[...]
    out_ref[...] = yn + bonus


def _wkv_scan(seqs, s0, lane_params):
    t_len, n, bh = seqs[0].shape
    tc = SCAN_CHUNK if t_len % SCAN_CHUNK == 0 else t_len
    assert t_len % tc == 0 and bh % LANES == 0
    seq_spec = pl.BlockSpec((tc, n, LANES), lambda l, t: (t, 0, l))
    st_spec = pl.BlockSpec((n, n, LANES), lambda l, t: (0, 0, l))
    par_spec = pl.BlockSpec((n, LANES), lambda l, t: (0, 0))
    return pl.pallas_call(
        _scan_kernel, grid=(bh // LANES, t_len // tc),
        in_specs=[seq_spec] * 5 + [st_spec] + [par_spec] * 5,
        out_specs=[seq_spec, st_spec],
        out_shape=[jax.ShapeDtypeStruct((t_len, n, bh), F32), jax.ShapeDtypeStruct((n, n, bh), F32)],
        scratch_shapes=[pltpu.VMEM((tc, n, LANES), F32)] * 5,
        compiler_params=_params(("arbitrary", "arbitrary")), name="wkv_scan")(*seqs, s0, *lane_params)


def _lambda(lq1_ref, lk1_ref, lq2_ref, lk2_ref, lam_init):
    s1 = jnp.sum(lq1_ref[...] * lk1_ref[...], axis=-1, keepdims=True)
    s2 = jnp.sum(lq2_ref[...] * lk2_ref[...], axis=-1, keepdims=True)
    return jnp.exp(s1) - jnp.exp(s2) + lam_init


def _prompt_attn_kernel(lam_init, dv, q_ref, k_ref, vt_ref, lq1_ref, lk1_ref, lq2_ref, lk2_ref,
                        subw_ref, o_ref, qbd_scr, m_scr, acc_scr):
    qi = pl.program_id(1)
    tq = q_ref.shape[1]
    n_heads, _, tk, dk = k_ref.shape[1:]
    heads = range(n_heads)

    for h in heads:
        qt = q_ref[0, :, h * dk:(h + 1) * dk].astype(F32).T
        sub = lax.broadcasted_iota(jnp.int32, qt.shape, 0)
        qbd_scr[h, :, 0:tq] = jnp.where(sub < DA_HEAD_DIM, qt, 0.0).astype(BF16)
        qbd_scr[h, :, tq:2 * tq] = jnp.where(sub >= DA_HEAD_DIM, qt, 0.0).astype(BF16)
    m_scr[...] = jnp.full_like(m_scr, NEG_INF)
    acc_scr[...] = jnp.zeros_like(acc_scr)

    def step(kb, mask):
        ss = [jnp.dot(k_ref[0, h, kb], qbd_scr[h], preferred_element_type=F32) for h in heads]
        if mask is not None:
            ss = [jnp.where(mask, s, NEG_INF) for s in ss]
        m_prev = [m_scr[h] for h in heads]
        m_new = [jnp.maximum(m_prev[h], jnp.max(ss[h], axis=0, keepdims=True)) for h in heads]
        ps = [jnp.exp2(ss[h] - m_new[h]).astype(BF16) for h in heads]
        alpha = [jnp.exp2(m_prev[h] - m_new[h]) for h in heads]
        pv = [jnp.dot(vt_ref[0, h, kb], ps[h], preferred_element_type=F32) for h in heads]
        for h in heads:
            acc_scr[h] = alpha[h] * acc_scr[h] + pv[h]
            m_scr[h] = m_new[h]

    def body(kb, carry):
        step(kb, None)
        return carry

    lax.fori_loop(0, qi, body, 0)

    key = lax.broadcasted_iota(jnp.int32, (tk, 2 * tq), 0)
    qry = lax.broadcasted_iota(jnp.int32, (tk, 2 * tq), 1)
    step(qi, key <= jnp.where(qry >= tq, qry - tq, qry))

    lam = _lambda(lq1_ref, lk1_ref, lq2_ref, lk2_ref, lam_init)
    for h in heads:
        acc = acc_scr[h, 0:dv]
        l = acc_scr[h, dv:dv + 1]
        ot = acc[:, :tq] / l[:, :tq] - lam * (acc[:, tq:] / l[:, tq:])
        ot = ot * lax.rsqrt(jnp.mean(ot * ot, axis=0, keepdims=True) + NORM_EPS)
        o_ref[0, :, h * dv:(h + 1) * dv] = (
            ot.T * subw_ref[...] * (1.0 - lam_init)).astype(o_ref.dtype)


def _prompt_attention(qb, kb, vb, lam_params, subw, lam_init, n_heads):
    b, seq_len, _ = qb.shape
    tq = tk = ATTN_TILE
    dk = kb.shape[-1] // n_heads
    dv = vb.shape[-1] // n_heads
    nq = pl.cdiv(seq_len, tq)
    pad = nq * tk - seq_len
    kp = jnp.pad(kb, ((0, 0), (0, pad), (0, 0))).reshape(b, nq, tk, n_heads, dk).transpose(0, 3, 1, 2, 4)
    vt = jnp.pad(vb, ((0, 0), (0, pad), (0, 0))).reshape(b, nq, tk, n_heads, dv).transpose(0, 3, 1, 4, 2)
    ones = jnp.zeros((b, n_heads, nq, BF16_SUBLANES, tk), BF16).at[:, :, :, 0].set(1.0)
    vt = jnp.concatenate([vt, ones], axis=3)
    dvp = dv + BF16_SUBLANES
    small = [_full(p.shape) for p in lam_params] + [_full(subw.shape)]
    return pl.pallas_call(
        functools.partial(_prompt_attn_kernel, lam_init, dv),
        grid=(b, nq),
        in_specs=[pl.BlockSpec((1, tq, n_heads * dk), lambda bi, qi: (bi, qi, 0)),
                  pl.BlockSpec((1, n_heads, nq, tk, dk), lambda bi, qi: (bi, 0, 0, 0, 0)),
                  pl.BlockSpec((1, n_heads, nq, dvp, tk), lambda bi, qi: (bi, 0, 0, 0, 0))] + small,
        out_specs=pl.BlockSpec((1, tq, n_heads * dv), lambda bi, qi: (bi, qi, 0)),
        out_shape=jax.ShapeDtypeStruct((b, seq_len, n_heads * dv), BF16),
        scratch_shapes=[pltpu.VMEM((n_heads, dk, 2 * tq), BF16),
                        pltpu.VMEM((n_heads, 1, 2 * tq), F32),
                        pltpu.VMEM((n_heads, dvp, 2 * tq), F32)],
        compiler_params=_params(("arbitrary",) * 2), name="prompt_attention")(
            qb, kp, vt, *lam_params, subw)


def _sample_attn_kernel(n_heads, lam_init, pt_ref, q_ref, kn_ref, vn_ref, lq1_ref, lk1_ref, lq2_ref,
                        lk2_ref, subw_ref, *refs):
    del pt_ref
    pp = PAGES_PER_STEP
    k_pages, v_pages = refs[:pp], refs[pp:2 * pp]
    o_ref, qp_scr, bias_scr, m_scr, l_scr, acc_scr = refs[2 * pp:]
    j = pl.program_id(1)
    t_new = q_ref.shape[1]
    dk = kn_ref.shape[2]
    dv = vn_ref.shape[2]
    half = n_heads * t_new
    rows = 2 * half
    keys = bias_scr.shape[1]

    def row_head(shape):
        return lax.rem(lax.broadcasted_iota(jnp.int32, shape, 0) // t_new, n_heads)

    @pl.when(j == 0)
    def _():
        q = q_ref[0]
        lane = lax.broadcasted_iota(jnp.int32, (t_new, dk), 1)
        pieces = []
        for mp in range(2):
            keep = (lane >= DA_HEAD_DIM) if mp else (lane < DA_HEAD_DIM)
            for h in range(n_heads):
                pieces.append(jnp.where(keep, q[:, h * dk:(h + 1) * dk], 0.0))
        qp_scr[...] = jnp.concatenate(pieces, axis=0).astype(BF16)
        col = lax.broadcasted_iota(jnp.int32, (rows, keys), 1)
        bias_scr[...] = jnp.where(lax.rem(col, n_heads) == row_head((rows, keys)), 0.0, NEG_INF)
        m_scr[...] = jnp.full_like(m_scr, NEG_INF)
        l_scr[...] = jnp.zeros_like(l_scr)
        acc_scr[...] = jnp.zeros_like(acc_scr)

    def scores(k_blk):
        return lax.dot_general(qp_scr[...], k_blk.astype(BF16), (((1,), (1,)), ((), ())),
                               preferred_element_type=F32)

    def update(ss, vs):
        m_prev = m_scr[...]
        m_new = m_prev
        for s in ss:
            m_new = jnp.maximum(m_new, jnp.max(s, axis=-1, keepdims=True))
        alpha = jnp.exp2(m_prev - m_new)
        l_new = alpha * l_scr[...]
        acc = alpha * acc_scr[...]
        for s, v_blk in zip(ss, vs):
            p = jnp.exp2(s - m_new)
            l_new = l_new + jnp.sum(p, axis=-1, keepdims=True)
            acc = acc + jnp.dot(p.astype(BF16), v_blk.astype(BF16), preferred_element_type=F32)
        l_scr[...] = l_new
        acc_scr[...] = acc
        m_scr[...] = m_new

    bias = bias_scr[...]
    update([scores(k_pages[p][0]) + bias for p in range(pp)], [v_pages[p][0] for p in range(pp)])

    @pl.when(j == pl.num_programs(1) - 1)
    def _():
        nk = kn_ref.shape[1]
        col = lax.broadcasted_iota(jnp.int32, (rows, nk), 1)
        row = lax.broadcasted_iota(jnp.int32, (rows, nk), 0)
        ok = ((lax.rem(col, n_heads) == row_head((rows, nk)))
              & (col // n_heads <= lax.rem(row, t_new)))
        update([jnp.where(ok, scores(kn_ref[0]), NEG_INF)], [vn_ref[0]])
        lam = _lambda(lq1_ref, lk1_ref, lq2_ref, lk2_ref, lam_init)
        on = acc_scr[...] / l_scr[...]
        for h in range(n_heads):
            o = on[h * t_new:(h + 1) * t_new] - lam * on[half + h * t_new:half + (h + 1) * t_new]
            o_ref[0, :, h * dv:(h + 1) * dv] = _rms(o, subw_ref[...]) * (1.0 - lam_init)


def _sample_attention(q, k_new, v_new, cache_k, cache_v, page_table, lam_params, subw, lam_init,
                      n_heads):
    db, t_new, width = q.shape
    n_pages = page_table.shape[1]
    keys, dk = cache_k.shape[1], cache_k.shape[2]
    dv = cache_v.shape[2]
    pp = PAGES_PER_STEP
    assert n_pages % pp == 0
    rows = 2 * n_heads * t_new
    seq = lambda t: pl.BlockSpec((1,) + t.shape[1:], lambda b, j, pt: (b, 0, 0))
    small = [pl.BlockSpec(p.shape, lambda b, j, pt: (0, 0)) for p in (*lam_params, subw)]

    def page_spec(p, w):
        return pl.BlockSpec((1, keys, w), lambda b, j, pt: (pt[b, j * pp + p], 0, 0))

    grid_spec = pltpu.PrefetchScalarGridSpec(
        num_scalar_prefetch=1, grid=(db, n_pages // pp),
        in_specs=[seq(q), seq(k_new), seq(v_new)] + small
        + [page_spec(p, dk) for p in range(pp)] + [page_spec(p, dv) for p in range(pp)],
        out_specs=pl.BlockSpec((1, t_new, n_heads * dv), lambda b, j, pt: (b, 0, 0)),
        scratch_shapes=[pltpu.VMEM((rows, dk), BF16), pltpu.VMEM((rows, keys), F32),
                        pltpu.VMEM((rows, 1), F32), pltpu.VMEM((rows, 1), F32),
                        pltpu.VMEM((rows, dv), F32)])
    return pl.pallas_call(
        functools.partial(_sample_attn_kernel, n_heads, lam_init), grid_spec=grid_spec,
        out_shape=jax.ShapeDtypeStruct((db, t_new, n_heads * dv), F32),
        compiler_params=_params(("arbitrary", "arbitrary")), name="sample_attention")(
            page_table, q, k_new, v_new, *lam_params, subw, *([cache_k] * pp), *([cache_v] * pp))


def _merge_kernel(n_experts, x_ref, ya_ref, g_ref, o_ref, n1_ref, wg_ref, wr_ref, wd_ref, wo_ref,
                  n2_ref, rhi_ref, rlo_ref, rb_ref, x1_ref, h2_ref, route_ref):
    x = x_ref[...]
    d = x.shape[1]
    h = _rms(x, n1_ref[...])
    gates = jax.nn.sigmoid(jnp.dot(h.astype(BF16), wg_ref[...], preferred_element_type=F32))
    ya = jnp.dot((ya_ref[...] * g_ref[...]).astype(BF16), wr_ref[...], preferred_element_type=F32)
    od = jnp.dot(o_ref[...].astype(BF16), wd_ref[...], preferred_element_type=F32)
    merged = gates[:, :d] * ya + gates[:, d:] * od
    x1 = x + jnp.dot(merged.astype(BF16), wo_ref[...], preferred_element_type=F32)
    x1_ref[...] = x1
    h2 = _rms(x1, n2_ref[...])
    h2_hi = h2.astype(BF16)
    h2_ref[...] = h2_hi
    h2_lo = (h2 - h2_hi.astype(F32)).astype(BF16)
    logits = (jnp.dot(h2_hi, rhi_ref[...], preferred_element_type=F32)
              + jnp.dot(h2_lo, rhi_ref[...], preferred_element_type=F32)
              + jnp.dot(h2_hi, rlo_ref[...], preferred_element_type=F32) + rb_ref[...])
    lane = lax.broadcasted_iota(jnp.int32, logits.shape, 1)
    work = jnp.where(lane < n_experts, logits, -jnp.inf)
    vals, idxs = [], []
    for _ in range(TOP_K):
        m = jnp.max(work, axis=-1, keepdims=True)
        idx = jnp.min(jnp.where(work == m, lane, LANES), axis=-1, keepdims=True)
        vals.append(m)
        idxs.append(idx)
        work = jnp.where(lane == idx, -jnp.inf, work)
    es = [jnp.exp(v - vals[0]) for v in vals]
    denom = es[0] + es[1] + es[2] + es[3]
    route = jnp.zeros(logits.shape, F32)
    for kk in range(TOP_K):
        route = jnp.where(lane == kk, es[kk] / denom, route)
        route = jnp.where(lane == TOP_K + kk, idxs[kk].astype(F32), route)
    route_ref[...] = route


def _merge_and_route(x, ya, g, o, n1, wg, wr, wd, wo, n2, rhi, rlo, rb, n_experts):
    n, d = x.shape
    tm = min(TOKEN_TILE, n)
    assert n % tm == 0
    tok = lambda w: pl.BlockSpec((tm, w), lambda i: (i, 0))
    weights = (n1, wg, wr, wd, wo, n2, rhi, rlo, rb)
    return pl.pallas_call(
        functools.partial(_merge_kernel, n_experts), grid=(n // tm,),
        in_specs=[tok(d), tok(ya.shape[1]), tok(g.shape[1]), tok(o.shape[1])]
        + [_full(w.shape) for w in weights],
        out_specs=[tok(d), tok(d), tok(LANES)],
        out_shape=[jax.ShapeDtypeStruct((n, d), F32), jax.ShapeDtypeStruct((n, d), BF16),
                   jax.ShapeDtypeStruct((n, LANES), F32)],
        compiler_params=_params(("arbitrary",)), name="merge_and_route")(x, ya, g, o, *weights)


def _expert_kernel(blk_e_ref, n_used_ref, rows_ref, wgu_ref, bgu_ref, wdn_ref, bdn_ref, out_ref):
    del blk_e_ref
    i = pl.program_id(0)
    f = wdn_ref.shape[1]

    @pl.when(i < n_used_ref[0])
    def _():
        gu = jnp.dot(rows_ref[...], wgu_ref[0], preferred_element_type=F32) + bgu_ref[0]
        gate = jnp.minimum(gu[:, :f], SWIGLU_LIMIT)
        up = jnp.clip(gu[:, f:], -SWIGLU_LIMIT, SWIGLU_LIMIT)
        glu = gate * jax.nn.sigmoid(gate * SWIGLU_ALPHA)
        mid = ((up + 1.0) * glu).astype(BF16)
        out_ref[...] = jnp.dot(mid, wdn_ref[0], preferred_element_type=F32) + bdn_ref[0]

    @pl.when(i >= n_used_ref[0])
    def _():
        out_ref[...] = jnp.zeros_like(out_ref)


def _expert_blocks(rows, blk_e, n_used, wgu, bgu, wdn, bdn, bm):
    n_rows, d = rows.shape
    n_blk = n_rows // bm
    f2, f = wgu.shape[2], wdn.shape[1]
    grid_spec = pltpu.PrefetchScalarGridSpec(
        num_scalar_prefetch=2, grid=(n_blk,),
        in_specs=[pl.BlockSpec((bm, d), lambda i, be, nu: (i, 0)),
                  pl.BlockSpec((1, d, f2), lambda i, be, nu: (be[i], 0, 0)),
                  pl.BlockSpec((1, 1, f2), lambda i, be, nu: (be[i], 0, 0)),
                  pl.BlockSpec((1, f, d), lambda i, be, nu: (be[i], 0, 0)),
                  pl.BlockSpec((1, 1, d), lambda i, be, nu: (be[i], 0, 0))],
        out_specs=pl.BlockSpec((bm, d), lambda i, be, nu: (i, 0)))
    return pl.pallas_call(
        _expert_kernel, grid_spec=grid_spec, out_shape=jax.ShapeDtypeStruct((n_rows, d), F32),
        compiler_params=_params(("arbitrary",)), name="expert_blocks")(
            blk_e, n_used, rows, wgu, bgu, wdn, bdn)


def _moe(h2, route, wgu, bgu, wdn, bdn, n_experts):
    n, d = h2.shape
    a = n * TOP_K
    bm = min(MOE_ROWS, a // n_experts)
    flat_e = route[:, TOP_K:2 * TOP_K].astype(jnp.int32).reshape(-1)
    iota_a = jnp.arange(a, dtype=jnp.int32)
    se, order = lax.sort_key_val(flat_e, iota_a)
    counts = jnp.sum((flat_e[:, None] == jnp.arange(n_experts, dtype=jnp.int32)[None, :])
                     .astype(jnp.int32), axis=0)
    starts = jnp.cumsum(counts) - counts
    padded = (counts + bm - 1) // bm * bm
    pad_end = jnp.cumsum(padded)
    pad_start = pad_end - padded
    n_blk = -(-a // bm) + n_experts
    dest_sorted = (pad_start[se] + iota_a - starts[se]).astype(jnp.int32)
    _, dest = lax.sort_key_val(order, dest_sorted)
    blk_start = jnp.arange(n_blk, dtype=jnp.int32) * bm
    blk_e = jnp.minimum(jnp.sum((pad_end[None, :] <= blk_start[:, None]).astype(jnp.int32), axis=1),
                        n_experts - 1).astype(jnp.int32)
    n_used = (pad_end[-1] // bm).astype(jnp.int32).reshape(1)
    row_id = jnp.arange(n_blk * bm, dtype=jnp.int32)
    row_e = jnp.repeat(blk_e, bm)
    sorted_pos = jnp.clip(starts[row_e] + row_id - pad_start[row_e], 0, a - 1)
    src_tok = order[sorted_pos] // TOP_K
    rows = h2.at[src_tok].get(mode='promise_in_bounds')
    out = _expert_blocks(rows, blk_e, n_used, wgu, bgu, wdn, bdn, bm)
    dest_t = dest.reshape(n, TOP_K).T.reshape(-1)
    return out.at[dest_t].get(mode='promise_in_bounds').reshape(TOP_K, n, d)


def _final_kernel(x_ref, p_ref, route_ref, w_ref, y_ref):
    x = x_ref[...]
    route = route_ref[...]
    for kk in range(TOP_K):
        x = x + route[:, kk:kk + 1] * p_ref[kk]
    y_ref[...] = _rms(x, w_ref[...])


def _final_norm(x1, picked, route, w):
    n, d = x1.shape
    tm = min(TOKEN_TILE, n)
    tok = pl.BlockSpec((tm, d), lambda i: (i, 0))
    return pl.pallas_call(
        _final_kernel, grid=(n // tm,),
        in_specs=[tok, pl.BlockSpec((TOP_K, tm, d), lambda i: (0, i, 0)),
                  pl.BlockSpec((tm, LANES), lambda i: (i, 0)), _full(w.shape)],
        out_specs=tok, out_shape=jax.ShapeDtypeStruct((n, d), F32),
        compiler_params=_params(("arbitrary",)), name="final_norm")(x1, picked, route, w)


def _to_scan_layout(t, b, t_len, heads):
    return t.reshape(b, t_len, heads, RW_HEAD_DIM).transpose(1, 3, 0, 2).reshape(
        t_len, RW_HEAD_DIM, b * heads)


def _lane_param(p, heads):
    return jnp.tile(p.reshape(heads, RW_HEAD_DIM).T, (1, LANES // heads))


def _group(x, shift_prev, wkv0, attend, w, lam_init):
    b, t_len, d = x.shape
    n = b * t_len
    rw, rd, ri, rg, qk, vw = w['dims']
    heads = rw // RW_HEAD_DIM
    xf = x.reshape(n, d)
    if shift_prev is None:
        start_mask = start_vals = None
    else:
        c1 = shift_prev.shape[1]
        start_mask = (jnp.arange(n) % t_len == 0).astype(F32).reshape(n, 1)
        start_vals = jnp.zeros((b, t_len, c1), F32).at[:, 0].set(shift_prev).reshape(n, c1)
    (pr, r, dec, kx, vx, a, g, k, v, qb, kb, vb) = _in_projection(
        xf, t_len, start_mask, start_vals, w['n1'], w['w_main'], w['mu'], w['w0'], w['w2'], w['a0'],
        w['a2'], w['g2'], w['dims'], w['n_heads'])

    seqs = [_to_scan_layout(s, b, t_len, heads) for s in (r, dec, kx, vx, a)]
    if wkv0 is None:
        s0 = jnp.zeros((RW_HEAD_DIM, RW_HEAD_DIM, b * heads), F32)
    else:
        s0 = wkv0.transpose(3, 2, 0, 1).reshape(RW_HEAD_DIM, RW_HEAD_DIM, b * heads)
    ya_t, s_t = _wkv_scan(seqs, s0, w['lane_params'])
    ya = ya_t.reshape(t_len, RW_HEAD_DIM, b, heads).transpose(2, 0, 3, 1).reshape(n, rw)
    wkv_new = s_t.reshape(RW_HEAD_DIM, RW_HEAD_DIM, b, heads).transpose(2, 3, 1, 0)

    o = attend(qb, kb, vb, k, v)
    x1, h2, route = _merge_and_route(xf, ya, g, o.reshape(n, vw), w['n1'], w['wg'], w['wr'], w['wd'],
                                     w['wo'], w['n2'], w['rhi'], w['rlo'], w['rb'], w['n_experts'])
    picked = _moe(h2, route, w['wgu'], w['bgu'], w['wdn'], w['bdn'], w['n_experts'])
    y = _final_norm(x1, picked, route, w['nf'])
    shift_new = pr.reshape(b, t_len, -1)[:, -1]
    return y.reshape(b, t_len, d), k, v, wkv_new, shift_new


def kernel(x_prompt, x_sample, cache_k, cache_v, state_wkv, state_shift, page_table, meta_tokens, norm1_w, w_in, rw_mu, rw_w0, rw_w2, rw_a0, rw_a2, rw_g2, rw_k_k, rw_k_a, rw_r_k, rw_ln_w, rw_ln_b, w_out_rwkv, da_lq1, da_lk1, da_lq2, da_lk2, da_subln_w, w_out_diff, w_out, norm2_w, w_router, b_router, w_gu, b_gu, w_down, b_down, norm_f_w):
    depth = w_in.shape[0]
    assert depth == 1
    b, seq, d = x_prompt.shape
    db, dseq, _ = x_sample.shape
    rw = rw_w0.shape[1]
    rd, ri, rg = rw_w2.shape[1], rw_a2.shape[1], rw_g2.shape[1]
    n_pool, page, n_heads, dk = cache_k.shape[1:]
    dv = cache_v.shape[4]
    qk = n_heads * dk
    vw = n_heads * dv
    dims = (rw, rd, ri, rg, qk, vw)
    c1 = 3 * rw + rd + ri + rg
    c4 = c1 + 2 * qk + vw
    heads = rw // RW_HEAD_DIM
    n_experts = w_router.shape[2]
    l = 0
    lam_init = 0.8 - 0.6 * math.exp(-0.3 * l)
    row = lambda p: p.reshape(1, -1)

    r_pad = jnp.pad(w_router[l], ((0, 0), (0, LANES - n_experts)))
    r_hi = r_pad.astype(BF16)
    w = dict(
        dims=dims, n_experts=n_experts, n_heads=n_heads,
        n1=row(norm1_w[l]), w_main=w_in[l][:, :c4].astype(BF16), mu=row(rw_mu[l]),
        w0=row(rw_w0[l]), w2=rw_w2[l].astype(BF16), a0=row(rw_a0[l]), a2=rw_a2[l].astype(BF16),
        g2=rw_g2[l].astype(BF16),
        lane_params=[_lane_param(p[l], heads) for p in (rw_k_k, rw_k_a, rw_r_k, rw_ln_w, rw_ln_b)],
        wg=w_in[l][:, c4:].astype(BF16), wr=w_out_rwkv[l].astype(BF16),
        wd=w_out_diff[l].astype(BF16), wo=w_out[l].astype(BF16), n2=row(norm2_w[l]),
        rhi=r_hi, rlo=(r_pad - r_hi.astype(F32)).astype(BF16),
        rb=jnp.pad(b_router[l], (0, LANES - n_experts)).reshape(1, LANES),
        wgu=w_gu[l].astype(BF16), bgu=b_gu[l].reshape(n_experts, 1, -1),
        wdn=w_down[l].astype(BF16), bdn=b_down[l].reshape(n_experts, 1, -1), nf=row(norm_f_w))
    lam_params = [row(p[l]) for p in (da_lq1, da_lk1, da_lq2, da_lk2)]
    subw = row(da_subln_w[l])

    meta = jnp.broadcast_to(meta_tokens[None], (b, N_META, d))
    xp = jnp.concatenate([meta, x_prompt], axis=1)
    lp = seq + N_META

    def attend_prompt(qb, kb, vb, k, v):
        del k, v
        shp = lambda t: t.reshape(b, lp, -1)
        return _prompt_attention(shp(qb), shp(kb), shp(vb), lam_params, subw, lam_init, n_heads)

    ck = cache_k.reshape(n_pool, page * n_heads, dk)
    cv = cache_v.reshape(n_pool, page * n_heads, dv)

    def attend_sample(qb, kb, vb, k, v):
        del kb, vb
        rows = dseq * n_heads
        pad = lambda t, w_: jnp.pad(t.reshape(db, rows, w_), ((0, 0), (0, LANES - rows), (0, 0)))
        return _sample_attention(qb.astype(F32).reshape(db, dseq, qk), pad(k, dk), pad(v, dv), ck, cv,
                                 page_table, lam_params, subw, lam_init, n_heads)

    yp, kp, vp, wp, sp = _group(xp, None, None, attend_prompt, w, lam_init)
    ys, ks, vs, ws, ss = _group(x_sample, state_shift[l], state_wkv[l], attend_sample, w, lam_init)

    hd = lambda t, bb, tt: t.reshape(1, bb, tt, n_heads, -1)
    return (yp[:, N_META:], ys, hd(kp, b, lp), hd(vp, b, lp), wp[None], sp[None],
            hd(ks, db, dseq), hd(vs, db, dseq), ws[None], ss[None])
```
